```python
import jax, jax.numpy as jnp
from jax import lax
import numpy as np

D_MODEL = 1024
BATCH = 32
SEQ = 2048
DEPTH = 2
DEC_BATCH = 8
DEC_SEQ = 8192
PAST_LEN = 128

GRID_W = 64
N_GROUPS = 4
GROUP_W = D_MODEL // N_GROUPS
GLA_HEADS = 4
GLA_DK = GROUP_W // 2 // GLA_HEADS
GLA_DV = GROUP_W // GLA_HEADS
GLA_GATE_RANK = 16
GLA_GATE_NORM = 16.0
GLA_CHUNK = 64
POOL_WINDOWS = (2, 4, 8, 16)
POOL_GW = GROUP_W // len(POOL_WINDOWS)
NA_HEADS = 4
NA_DH = GROUP_W // NA_HEADS
NA_WIN_R = 8
NA_WIN_C = 16
NA_COL_BLOCK = 16
NA_KEY_COLS = 32
MLA_HEADS = 4
MLA_NOPE = 64
MLA_ROPE = 32
MLA_V = GROUP_W // MLA_HEADS
MLA_Q_RANK = 256
MLA_KV_RANK = 128
MLA_SCALE = (MLA_NOPE + MLA_ROPE) ** -0.5
ROPE_THETA = 10000.0
Q_BLOCK = 128
D_FF = 2816
N_MOD = 9
EPS = 1e-6
NEG_INF = -1e30
F32 = jnp.float32

PROJ_SIZES = (
    GLA_HEADS * GLA_DK, GLA_HEADS * GLA_DK, GLA_HEADS * GLA_DV, GLA_HEADS * GLA_DV,
    GLA_GATE_RANK, GLA_GATE_RANK,
    GROUP_W,
    NA_HEADS * NA_DH, NA_HEADS * NA_DH, NA_HEADS * NA_DH,
    MLA_Q_RANK, MLA_KV_RANK, MLA_ROPE,
)
PROJ_W = sum(PROJ_SIZES)

kernel_name = "hybrid_bidir_encoder_hymba4"


def rmsnorm(x, w):
    xf = x.astype(F32)
    y = xf * lax.rsqrt(jnp.mean(xf * xf, axis=-1, keepdims=True) + EPS)
    return (y * w.astype(F32)).astype(x.dtype)


def swiglu(h, wg, wu, wd):
    return (jax.nn.silu(h @ wg) * (h @ wu)) @ wd


def gla_chunked(q, k, v, g, inclusive):
    B, H, S, DK = q.shape
    DV = v.shape[-1]
    C = GLA_CHUNK
    N = S // C
    q = q.astype(F32).reshape(B, H, N, C, DK)
    k = k.astype(F32).reshape(B, H, N, C, DK)
    v = v.astype(F32).reshape(B, H, N, C, DV)
    b = jnp.cumsum(g.astype(F32).reshape(B, H, N, C, DK), axis=3)
    q_dec = q * jnp.exp(b)
    attn = jnp.einsum('bhnid,bhnjd->bhnij', q_dec, k * jnp.exp(-b))
    mask = np.tril(np.ones((C, C), dtype=bool), 0 if inclusive else -1)
    attn = jnp.where(mask, attn, 0.0)
    o = jnp.einsum('bhnij,bhnjv->bhniv', attn, v)
    b_last = b[:, :, :, -1:, :]
    kv_chunk = jnp.einsum('bhncd,bhncv->nbhdv', k * jnp.exp(b_last - b), v)
    dec_chunk = jnp.moveaxis(jnp.exp(b_last[:, :, :, 0, :]), 2, 0)

    def step(state, inp):
        kv_n, dec_n = inp
        return dec_n[..., None] * state + kv_n, state

    _, prev = lax.scan(step, jnp.zeros((B, H, DK, DV), F32), (kv_chunk, dec_chunk))
    o = o + jnp.einsum('bhncd,nbhdv->bhncv', q_dec, prev)
    return o.reshape(B, H, S, DV)


def gla_mixer(q, k, v, g_out, lr_f, lr_b, w_f, b_f, w_b, b_b, w_norm):
    B, S, _ = q.shape

    def heads(t, d):
        return t.reshape(B, S, GLA_HEADS, d).transpose(0, 2, 1, 3)

    def log_decay(lr, w, bias):
        z = (lr @ w + bias).astype(F32)
        return heads(jax.nn.log_sigmoid(z) / GLA_GATE_NORM, GLA_DK)

    qh = heads(q, GLA_DK) * (GLA_DK ** -0.5)
    kh = heads(k, GLA_DK)
    vh = heads(v, GLA_DV)
    o_fwd = gla_chunked(qh, kh, vh, log_decay(lr_f, w_f, b_f), True)
    flip = lambda t: jnp.flip(t, axis=2)
    o_bwd = flip(gla_chunked(flip(qh), flip(kh), flip(vh), flip(log_decay(lr_b, w_b, b_b)), False))
    o = rmsnorm((o_fwd + o_bwd).transpose(0, 2, 1, 3), w_norm).reshape(B, S, GLA_HEADS * GLA_DV)
    return (o * jax.nn.silu(g_out.astype(F32))).astype(q.dtype)


def pool_mixer(u, w_pool, scale):
    B, S, _ = u.shape
    uf = u.astype(F32)
    cs = jnp.pad(jnp.cumsum(uf, axis=1), ((0, 0), (1, 0), (0, 0)))
    t = jnp.arange(S)
    outs = []
    for gi, w in enumerate(POOL_WINDOWS):
        lo = jnp.clip(t - w // 2, 0, S)
        hi = jnp.clip(t + w // 2, 0, S)
        sl = slice(gi * POOL_GW, (gi + 1) * POOL_GW)
        csg = cs[:, :, sl]
        mean = (jnp.take(csg, hi, axis=1) - jnp.take(csg, lo, axis=1)) / (hi - lo).astype(F32)[None, :, None]
        outs.append(jnp.einsum('bsc,cd->bsd', mean - uf[:, :, sl], w_pool[gi].astype(F32)))
    return (jnp.concatenate(outs, axis=-1) * scale.astype(F32)).astype(u.dtype)


def na_mixer(q, k, v, rpb):
    B, S, _ = q.shape
    rows = S // GRID_W
    win_r = min(NA_WIN_R, rows)
    n_cb = GRID_W // NA_COL_BLOCK
    shp = (B, rows, GRID_W, NA_HEADS, NA_DH)
    qg = q.reshape(shp) * (NA_DH ** -0.5)
    kg = k.reshape(shp)
    vg = v.reshape(shp)
    qcol = np.arange(GRID_W).reshape(n_cb, NA_COL_BLOCK)
    kstart = np.clip(np.arange(n_cb) * NA_COL_BLOCK - NA_WIN_C // 2, 0, GRID_W - NA_KEY_COLS)
    kcol = kstart[:, None] + np.arange(NA_KEY_COLS)[None, :]
    qstart = np.clip(qcol - NA_WIN_C // 2, 0, GRID_W - NA_WIN_C)
    col_ok = (kcol[:, None, :] >= qstart[:, :, None]) & (kcol[:, None, :] < qstart[:, :, None] + NA_WIN_C)
    dcol_idx = np.clip(kcol[:, None, :] - qcol[:, :, None] + NA_WIN_C - 1, 0, 2 * NA_WIN_C - 2)
    col_ok = jnp.asarray(col_ok)[:, :, None, :]
    dcol_idx = jnp.asarray(dcol_idx)[:, :, None, :]
    kb_all = kg[:, :, kcol]
    vb_all = vg[:, :, kcol]

    def row_fn(r):
        r0 = jnp.clip(r - win_r // 2, 0, rows - win_r)
        kb = lax.dynamic_slice_in_dim(kb_all, r0, win_r, axis=1)
        vb = lax.dynamic_slice_in_dim(vb_all, r0, win_r, axis=1)
        qr = lax.dynamic_index_in_dim(qg, r, axis=1, keepdims=False)
        qr = qr.reshape(B, n_cb, NA_COL_BLOCK, NA_HEADS, NA_DH)
        s = jnp.einsum('bnqhd,brnkhd->bhnqrk', qr, kb).astype(F32)
        drow_idx = (r0 + jnp.arange(win_r) - r + NA_WIN_R - 1)[None, None, :, None]
        bias = rpb[:, drow_idx, dcol_idx].astype(F32)
        s = jnp.where(col_ok, s + bias, NEG_INF)
        p = jax.nn.softmax(s, axis=(-2, -1)).astype(vb.dtype)
        o = jnp.einsum('bhnqrk,brnkhd->bnqhd', p, vb)
        return o.reshape(B, GRID_W, NA_HEADS * NA_DH)

    out = lax.map(row_fn, jnp.arange(rows))
    return jnp.moveaxis(out, 0, 1).reshape(B, S, NA_HEADS * NA_DH)


def rope(x, pos):
    half = MLA_ROPE // 2
    inv = ROPE_THETA ** (-jnp.arange(half, dtype=F32) / half)
    ang = pos[:, None] * inv[None, :]
    cos = jnp.cos(ang)[None, :, None, :]
    sin = jnp.sin(ang)[None, :, None, :]
    xf = x.astype(F32)
    x1, x2 = xf[..., :half], xf[..., half:]
    return jnp.concatenate([x1 * cos - x2 * sin, x1 * sin + x2 * cos], axis=-1).astype(x.dtype)


def mla_mixer(c_q, c_kv, k_pe, w_qnorm, w_uq, w_kvnorm, w_ukv):
    B, S, _ = c_q.shape
    H = MLA_HEADS
    pos = jnp.arange(S, dtype=F32)
    q = (rmsnorm(c_q, w_qnorm) @ w_uq).reshape(B, S, H, MLA_NOPE + MLA_ROPE)
    kv = (rmsnorm(c_kv, w_kvnorm) @ w_ukv).reshape(B, S, H, MLA_NOPE + MLA_V)
    q = jnp.concatenate([q[..., :MLA_NOPE], rope(q[..., MLA_NOPE:], pos)], axis=-1) * MLA_SCALE
    k_rot = jnp.broadcast_to(rope(k_pe[:, :, None, :], pos), (B, S, H, MLA_ROPE))
    k = jnp.concatenate([kv[..., :MLA_NOPE], k_rot], axis=-1)
    v = kv[..., MLA_NOPE:]
    nb = S // Q_BLOCK
    qb = jnp.moveaxis(q.reshape(B, nb, Q_BLOCK, H, MLA_NOPE + MLA_ROPE), 1, 0)

    def block(qi):
        s = jnp.einsum('bqhd,bkhd->bhqk', qi, k).astype(F32)
        p = jax.nn.softmax(s, axis=-1).astype(v.dtype)
        return jnp.einsum('bhqk,bkhd->bqhd', p, v)

    o = lax.map(block, qb)
    return jnp.moveaxis(o, 0, 1).reshape(B, S, H * MLA_V)


def token_mixing(h, w_in, w_out, gla_wgk_f, gla_bgk_f, gla_wgk_b, gla_bgk_b, gla_norm,
                 pool_w, pool_scale, na_rpb, mla_qnorm, mla_wuq, mla_kvnorm, mla_wukv):
    proj = h @ w_in
    cuts = []
    acc = 0
    for size in PROJ_SIZES[:-1]:
        acc += size
        cuts.append(acc)
    (a_q, a_k, a_v, a_g, a_lrf, a_lrb, b_u, c_q, c_k, c_v, d_cq, d_ckv, d_kpe) = jnp.split(proj, cuts, axis=-1)
    y_a = gla_mixer(a_q, a_k, a_v, a_g, a_lrf, a_lrb, gla_wgk_f, gla_bgk_f, gla_wgk_b, gla_bgk_b, gla_norm)
    y_b = pool_mixer(b_u, pool_w, pool_scale)
    y_c = na_mixer(c_q, c_k, c_v, na_rpb)
    y_d = mla_mixer(d_cq, d_ckv, d_kpe, mla_qnorm, mla_wuq, mla_kvnorm, mla_wukv)
    y = jnp.concatenate([y_a.astype(h.dtype), y_b.astype(h.dtype), y_c.astype(h.dtype), y_d.astype(h.dtype)], axis=-1)
    return y @ w_out


def encoder_layer(x, c, ada_w, ada_b, norm_ffn1, ffn1_wg, ffn1_wu, ffn1_wd, norm_mix, w_in,
                  gla_wgk_f, gla_bgk_f, gla_wgk_b, gla_bgk_b, gla_norm, pool_w, pool_scale, na_rpb,
                  mla_qnorm, mla_wuq, mla_kvnorm, mla_wukv, w_out, norm_ffn2, ffn2_wg, ffn2_wu, ffn2_wd):
    B = x.shape[0]
    mod = (jax.nn.silu(c) @ ada_w + ada_b).reshape(B, 1, N_MOD, D_MODEL)
    m = [mod[:, :, i, :] for i in range(N_MOD)]

    def modnorm(t, w, shift, scale):
        return rmsnorm(t, w) * (1 + scale) + shift

    x = x + 0.5 * (1 + m[2]) * swiglu(modnorm(x, norm_ffn1, m[0], m[1]), ffn1_wg, ffn1_wu, ffn1_wd)
    h = modnorm(x, norm_mix, m[3], m[4])
    x = x + (1 + m[5]) * token_mixing(h, w_in, w_out, gla_wgk_f, gla_bgk_f, gla_wgk_b, gla_bgk_b, gla_norm,
                                      pool_w, pool_scale, na_rpb, mla_qnorm, mla_wuq, mla_kvnorm, mla_wukv)
    x = x + 0.5 * (1 + m[8]) * swiglu(modnorm(x, norm_ffn2, m[6], m[7]), ffn2_wg, ffn2_wu, ffn2_wd)
    return x


def setup_inputs(seed: int = 0) -> dict:
    key = jax.random.key(seed)
    ks = iter(jax.random.split(key, 40))
    L, D, F = DEPTH, D_MODEL, D_FF

    def nrm(shape, s):
        return jax.random.normal(next(ks), shape, F32) * s

    def gain(shape):
        return 1.0 + nrm(shape, 0.05)

    return {
        "x_prompt": nrm((BATCH, SEQ, D), 1.0),
        "x_sample": nrm((DEC_BATCH, DEC_SEQ, D), 1.0),
        "c_prompt": nrm((BATCH, D), 1.0),
        "c_sample": nrm((DEC_BATCH, D), 1.0),
        "ada_w": nrm((L, D, N_MOD * D), 0.2 * D ** -0.5),
        "ada_b": nrm((L, N_MOD * D), 0.01),
        "norm_ffn1": gain((L, D)),
        "ffn1_wg": nrm((L, D, F), D ** -0.5),
        "ffn1_wu": nrm((L, D, F), D ** -0.5),
        "ffn1_wd": nrm((L, F, D), F ** -0.5),
        "norm_mix": gain((L, D)),
        "w_in": nrm((L, D, PROJ_W), D ** -0.5),
        "gla_wgk_f": nrm((L, GLA_GATE_RANK, GLA_HEADS * GLA_DK), GLA_GATE_RANK ** -0.5),
        "gla_bgk_f": nrm((L, GLA_HEADS * GLA_DK), 0.1),
        "gla_wgk_b": nrm((L, GLA_GATE_RANK, GLA_HEADS * GLA_DK), GLA_GATE_RANK ** -0.5),
        "gla_bgk_b": nrm((L, GLA_HEADS * GLA_DK), 0.1),
        "gla_norm": gain((L, GLA_DV)),
        "pool_w": nrm((L, len(POOL_WINDOWS), POOL_GW, POOL_GW), POOL_GW ** -0.5),
        "pool_scale": gain((L, GROUP_W)),
        "na_rpb": nrm((L, NA_HEADS, 2 * NA_WIN_R - 1, 2 * NA_WIN_C - 1), 0.1),
        "mla_qnorm": gain((L, MLA_Q_RANK)),
        "mla_wuq": nrm((L, MLA_Q_RANK, MLA_HEADS * (MLA_NOPE + MLA_ROPE)), MLA_Q_RANK ** -0.5),
        "mla_kvnorm": gain((L, MLA_KV_RANK)),
        "mla_wukv": nrm((L, MLA_KV_RANK, MLA_HEADS * (MLA_NOPE + MLA_V)), MLA_KV_RANK ** -0.5),
        "w_out": nrm((L, D, D), D ** -0.5),
        "norm_ffn2": gain((L, D)),
        "ffn2_wg": nrm((L, D, F), D ** -0.5),
        "ffn2_wu": nrm((L, D, F), D ** -0.5),
        "ffn2_wd": nrm((L, F, D), F ** -0.5),
        "final_norm": gain((D,)),
    }


def reference(x_prompt, x_sample, c_prompt, c_sample, ada_w, ada_b, norm_ffn1, ffn1_wg, ffn1_wu, ffn1_wd,
              norm_mix, w_in, gla_wgk_f, gla_bgk_f, gla_wgk_b, gla_bgk_b, gla_norm, pool_w, pool_scale,
              na_rpb, mla_qnorm, mla_wuq, mla_kvnorm, mla_wukv, w_out, norm_ffn2, ffn2_wg, ffn2_wu,
              ffn2_wd, final_norm):
    def run(x, c):
        for l in range(DEPTH):
            x = encoder_layer(x, c, ada_w[l], ada_b[l], norm_ffn1[l], ffn1_wg[l], ffn1_wu[l], ffn1_wd[l],
                              norm_mix[l], w_in[l], gla_wgk_f[l], gla_bgk_f[l], gla_wgk_b[l], gla_bgk_b[l],
                              gla_norm[l], pool_w[l], pool_scale[l], na_rpb[l], mla_qnorm[l], mla_wuq[l],
                              mla_kvnorm[l], mla_wukv[l], w_out[l], norm_ffn2[l], ffn2_wg[l], ffn2_wu[l],
                              ffn2_wd[l])
        return rmsnorm(x, final_norm)

    y_prompt = run(x_prompt, c_prompt)
    y_sample = run(x_sample, c_sample)
    return (y_prompt, y_sample)
```

```python
import functools

import numpy as np
import jax
import jax.numpy as jnp
from jax import lax
from jax.experimental import pallas as pl
from jax.experimental.pallas import tpu as pltpu

F32 = jnp.float32
BF16 = jnp.bfloat16

D_MODEL = 1024
DEPTH = 2
GRID_W = 64
GROUP_W = 256
GLA_HEADS = 4
GLA_DK = 32
GLA_DV = 64
GLA_GATE_RANK = 16
GLA_GATE_NORM = 16.0
GLA_CHUNK = 64
POOL_WINDOWS = (2, 4, 8, 16)
POOL_GW = 64
POOL_HALO = 16
NA_HEADS = 4
NA_DH = 64
NA_WIN_R = 8
NA_WIN_C = 16
NA_QROWS = 4
NA_KROWS = NA_QROWS + NA_WIN_R
MLA_HEADS = 4
MLA_NOPE = 64
MLA_ROPE = 32
MLA_V = 64
MLA_Q_RANK = 256
MLA_KV_RANK = 128
MLA_HEAD_PAD = 128
MLA_SCALE = (MLA_NOPE + MLA_ROPE) ** -0.5
ROPE_THETA = 10000.0
D_FF = 2816
N_MOD = 9
EPS = 1e-6
NEG_INF = -1e30

A_W = 896
A_LR = 768
U_OFF = A_W
C_OFF = U_OFF + GROUP_W
DQ_OFF = C_OFF + 3 * GROUP_W
DKV_OFF = DQ_OFF + MLA_Q_RANK
KPE_OFF = DKV_OFF + MLA_KV_RANK
PROJ_W2 = KPE_OFF + MLA_HEADS * MLA_HEAD_PAD

VMEM_LIMIT = 56 * 1024 * 1024

NT_DIMS = (((1,), (1,)), ((), ()))
TN_DIMS = (((0,), (0,)), ((), ()))


def _cparams(sem):
    return pltpu.CompilerParams(dimension_semantics=sem, vmem_limit_bytes=VMEM_LIMIT)


def _silu(x):
    return x / (1.0 + jnp.exp(-x))


def _rms(x, w):
    return x * lax.rsqrt(jnp.mean(x * x, axis=-1, keepdims=True) + EPS) * w


def _modnorm(x, w, shift, scale):
    return _rms(x, w) * (1.0 + scale) + shift


def _lane_mask(width, lo, hi):
    lane = lax.broadcasted_iota(jnp.int32, (1, width), 1)
    return (lane >= lo) & (lane < hi)


def _split_dot(ones_mat, x):
    hi = x.astype(BF16)
    lo = (x - hi.astype(F32)).astype(BF16)
    return (jnp.dot(ones_mat, hi, preferred_element_type=F32)
            + jnp.dot(ones_mat, lo, preferred_element_type=F32))


def _mod_kernel(c_ref, w_ref, b_ref, o_ref):
    s = _silu(c_ref[...]).astype(BF16)
    o_ref[...] = jnp.dot(s, w_ref[...].astype(BF16), preferred_element_type=F32) + b_ref[...]


def _modulation(c_all, ada_w, ada_b):
    n = c_all.shape[0]
    L, D, W = ada_w.shape
    tn = 1152
    return pl.pallas_call(
        _mod_kernel,
        grid=(L, W // tn),
        in_specs=[
            pl.BlockSpec((n, D), lambda l, j: (0, 0)),
            pl.BlockSpec((None, D, tn), lambda l, j: (l, 0, j)),
            pl.BlockSpec((None, 1, tn), lambda l, j: (l, 0, j)),
        ],
        out_specs=pl.BlockSpec((None, n, tn), lambda l, j: (l, 0, j)),
        out_shape=jax.ShapeDtypeStruct((L, n, W), F32),
        compiler_params=_cparams(("arbitrary", "arbitrary")),
        name="mod",
    )(c_all, ada_w, ada_b.reshape(L, 1, W))


def _ffn_kernel(x_ref, mod_ref, nw_ref, wg_ref, wu_ref, wd_ref, fw_ref, o_ref, *, mi, final):
    x = x_ref[...]
    shift = mod_ref[mi:mi + 1, :]
    scale = mod_ref[mi + 1:mi + 2, :]
    gate = mod_ref[mi + 2:mi + 3, :]
    h = _modnorm(x, nw_ref[...], shift, scale).astype(BF16)
    g = jnp.dot(h, wg_ref[...], preferred_element_type=F32)
    u = jnp.dot(h, wu_ref[...], preferred_element_type=F32)
    a = (_silu(g) * u).astype(BF16)
    y = jnp.dot(a, wd_ref[...], preferred_element_type=F32)
    out = x + (0.5 * (1.0 + gate)) * y
    if final:
        out = _rms(out, fw_ref[...])
    o_ref[...] = out


def _const_spec(shape):
    zeros = (0,) * len(shape)
    return pl.BlockSpec(shape, lambda b, i: zeros, pipeline_mode=pl.Buffered(1))


def _ffn(x, mod, nw, wg, wu, wd, fw, *, mi, final, tm=512):
    B, S, D = x.shape
    F = wg.shape[1]
    return pl.pallas_call(
        functools.partial(_ffn_kernel, mi=mi, final=final),
        grid=(B, S // tm),
        in_specs=[
            pl.BlockSpec((None, tm, D), lambda b, i: (b, i, 0)),
            pl.BlockSpec((None, N_MOD, D), lambda b, i: (b, 0, 0)),
            _const_spec((1, D)),
            _const_spec((D, F)),
            _const_spec((D, F)),
            _const_spec((F, D)),
            _const_spec((1, D)),
        ],
        out_specs=pl.BlockSpec((None, tm, D), lambda b, i: (b, i, 0)),
        out_shape=jax.ShapeDtypeStruct((B, S, D), F32),
        compiler_params=_cparams(("arbitrary", "arbitrary")),
        name="ffn",
    )(x, mod, nw, wg, wu, wd, fw)


def _rope(x, cos, sins):
    n = x.shape[1]
    lane = lax.broadcasted_iota(jnp.int32, (1, n), 1) % MLA_HEAD_PAD
    first_half = (lane >= MLA_NOPE) & (lane < MLA_NOPE + MLA_ROPE // 2)
    partner = jnp.where(first_half, pltpu.roll(x, n - MLA_ROPE // 2, 1), pltpu.roll(x, MLA_ROPE // 2, 1))
    return x * cos + partner * sins


def _proj_kernel(x_ref, mod_ref, nw_ref, win_ref, qn_ref, wuq_ref, kvn_ref, wuk_ref, wuv_ref,
                 cos_ref, sin_ref, a_ref, u_ref, c_ref, dq_ref, dk_ref, dv_ref):
    x = x_ref[...]
    h = _modnorm(x, nw_ref[...], mod_ref[3:4, :], mod_ref[4:5, :]).astype(BF16)
    p = jnp.dot(h, win_ref[...], preferred_element_type=F32)
    a_ref[...] = p[:, 0:A_W].astype(BF16)
    u_ref[...] = p[:, U_OFF:U_OFF + GROUP_W].astype(BF16)
    c_ref[:, 0:GROUP_W] = (p[:, C_OFF:C_OFF + GROUP_W] * (NA_DH ** -0.5)).astype(BF16)
    c_ref[:, GROUP_W:3 * GROUP_W] = p[:, C_OFF + GROUP_W:C_OFF + 3 * GROUP_W].astype(BF16)
    qn = _rms(p[:, DQ_OFF:DQ_OFF + MLA_Q_RANK], qn_ref[...]).astype(BF16)
    kvn = _rms(p[:, DKV_OFF:DKV_OFF + MLA_KV_RANK], kvn_ref[...]).astype(BF16)
    q = jnp.dot(qn, wuq_ref[...], preferred_element_type=F32)
    k = jnp.dot(kvn, wuk_ref[...], preferred_element_type=F32) + p[:, KPE_OFF:PROJ_W2]
    v = jnp.dot(kvn, wuv_ref[...], preferred_element_type=F32)
    cos = jnp.concatenate([cos_ref[...]] * MLA_HEADS, axis=1)
    sins = jnp.concatenate([sin_ref[...]] * MLA_HEADS, axis=1)
    dq_ref[...] = (_rope(q, cos, sins) * MLA_SCALE).astype(BF16)
    dk_ref[...] = _rope(k, cos, sins).astype(BF16)
    dv_ref[...] = v.astype(BF16)


def _proj(x, mod, nw, win, qn, wuq, kvn, wuk, wuv, cos, sins, *, tm=512):
    B, S, D = x.shape
    HP = MLA_HEADS * MLA_HEAD_PAD
    tok = lambda w: pl.BlockSpec((None, tm, w), lambda b, i: (b, i, 0))
    shapes = [A_W, GROUP_W, 3 * GROUP_W, HP, HP, GROUP_W]
    return pl.pallas_call(
        _proj_kernel,
        grid=(B, S // tm),
        in_specs=[
            tok(D),
            pl.BlockSpec((None, N_MOD, D), lambda b, i: (b, 0, 0)),
            _const_spec((1, D)),
            _const_spec((D, PROJ_W2)),
            _const_spec((1, MLA_Q_RANK)),
            _const_spec((MLA_Q_RANK, HP)),
            _const_spec((1, MLA_KV_RANK)),
            _const_spec((MLA_KV_RANK, HP)),
            _const_spec((MLA_KV_RANK, GROUP_W)),
            pl.BlockSpec((tm, MLA_HEAD_PAD), lambda b, i: (i, 0)),
            pl.BlockSpec((tm, MLA_HEAD_PAD), lambda b, i: (i, 0)),
        ],
        out_specs=[tok(w) for w in shapes],
        out_shape=[jax.ShapeDtypeStruct((B, S, w), BF16) for w in shapes],
        compiler_params=_cparams(("arbitrary", "arbitrary")),
        name="proj",
    )(x, mod, nw, win, qn, wuq, kvn, wuk, wuv, cos, sins)


def _log_sigmoid(z):
    return -(jnp.maximum(-z, 0.0) + jnp.log(1.0 + jnp.exp(-jnp.abs(z))))


def _gla_kernel(a_ref, wgk_ref, bgk_ref, gn_ref, o_ref, ofwd_ref, st_ref, *, nt, tt):
    j = pl.program_id(1)
    C = GLA_CHUNK
    nc = tt // C
    KW = GLA_HEADS * GLA_DK
    VW = GLA_HEADS * GLA_DV
    qscale = GLA_DK ** -0.5

    @pl.when((j == 0) | (j == nt))
    def _():
        st_ref[...] = jnp.zeros_like(st_ref)

    row = lax.broadcasted_iota(jnp.int32, (tt, tt), 0)
    col = lax.broadcasted_iota(jnp.int32, (tt, tt), 1)
    cum_mat = jnp.where((row >= col) & (row // C == col // C), 1.0, 0.0).astype(BF16)
    srow = lax.broadcasted_iota(jnp.int32, (GLA_HEADS * C, C), 0) % C
    scol = lax.broadcasted_iota(jnp.int32, (GLA_HEADS * C, C), 1)
    st_mask = (lax.broadcasted_iota(jnp.int32, (VW, KW), 0) // GLA_DV
               == lax.broadcasted_iota(jnp.int32, (VW, KW), 1) // GLA_DK)
    kmasks = [_lane_mask(KW, h * GLA_DK, (h + 1) * GLA_DK) for h in range(GLA_HEADS)]
    vmasks = [_lane_mask(VW, h * GLA_DV, (h + 1) * GLA_DV) for h in range(GLA_HEADS)]

    lr = a_ref[:, A_LR:A_LR + 2 * GLA_GATE_RANK]

    def log_decay(lo):
        z = jnp.dot(lr, wgk_ref[:, lo:lo + KW], preferred_element_type=F32) + bgk_ref[:, lo:lo + KW]
        return _log_sigmoid(z) * (1.0 / GLA_GATE_NORM)

    def chunk_out(q_intra, k_intra, q_inter, k_state, decay, vc, keep):
        qs = jnp.concatenate([jnp.where(m, q_intra, 0.0) for m in kmasks], axis=0).astype(BF16)
        att = lax.dot_general(qs, k_intra.astype(BF16), NT_DIMS, preferred_element_type=F32)
        att = jnp.where(keep, att, 0.0).astype(BF16)
        full = jnp.dot(att, vc, preferred_element_type=F32)
        o = jnp.zeros((C, VW), F32)
        for h in range(GLA_HEADS):
            o = o + jnp.where(vmasks[h], full[h * C:(h + 1) * C, :], 0.0)
        st = st_ref[...]
        o = o + lax.dot_general(q_inter.astype(BF16), st.astype(BF16), NT_DIMS,
                                preferred_element_type=F32)
        kv = lax.dot_general(vc, k_state.astype(BF16), TN_DIMS, preferred_element_type=F32)
        st_ref[...] = st * decay + jnp.where(st_mask, kv, 0.0)
        return o

    @pl.when(j < nt)
    def _():
        g = log_decay(0)
        b = _split_dot(cum_mat, g)
        for c in range(nc):
            r0 = c * C
            bc = b[r0:r0 + C, :]
            bl = b[r0 + C - 1:r0 + C, :]
            qc = a_ref[r0:r0 + C, 0:KW].astype(F32)
            kc = a_ref[r0:r0 + C, KW:2 * KW].astype(F32)
            vc = a_ref[r0:r0 + C, 2 * KW:2 * KW + VW]
            qd = qc * (jnp.exp(bc) * qscale)
            o = chunk_out(qd, kc * jnp.exp(-bc), qd, kc * jnp.exp(bl - bc), jnp.exp(bl), vc,
                          srow >= scol)
            ofwd_ref[pl.ds(pl.multiple_of(j * tt + r0, C), C), :] = o

    @pl.when(j >= nt)
    def _():
        t = 2 * nt - 1 - j
        g = log_decay(KW)
        binc = _split_dot(cum_mat, g)
        e = binc - g
        outs = [None] * nc
        for c in reversed(range(nc)):
            r0 = c * C
            ec = e[r0:r0 + C, :]
            gt = binc[r0 + C - 1:r0 + C, :]
            qc = a_ref[r0:r0 + C, 0:KW].astype(F32)
            kc = a_ref[r0:r0 + C, KW:2 * KW].astype(F32)
            vc = a_ref[r0:r0 + C, 2 * KW:2 * KW + VW]
            ke = kc * jnp.exp(ec)
            o = chunk_out(qc * (jnp.exp(-ec) * qscale), ke, qc * (jnp.exp(gt - ec) * qscale), ke,
                          jnp.exp(gt), vc, srow < scol)
            outs[c] = o + ofwd_ref[pl.ds(pl.multiple_of(t * tt + r0, C), C), :]
        o = jnp.concatenate(outs, axis=0)
        seg = jnp.where(lax.broadcasted_iota(jnp.int32, (VW, VW), 0) // GLA_DV
                        == lax.broadcasted_iota(jnp.int32, (VW, VW), 1) // GLA_DV, 1.0, 0.0).astype(BF16)
        ms = _split_dot_rhs(o * o, seg) * (1.0 / GLA_DV)
        gout = a_ref[:, 2 * KW + VW:2 * KW + 2 * VW].astype(F32)
        o_ref[...] = (o * lax.rsqrt(ms + EPS) * gn_ref[...] * _silu(gout)).astype(BF16)


def _split_dot_rhs(x, ones_mat):
    hi = x.astype(BF16)
    lo = (x - hi.astype(F32)).astype(BF16)
    return (jnp.dot(hi, ones_mat, preferred_element_type=F32)
            + jnp.dot(lo, ones_mat, preferred_element_type=F32))


def _gla(a, wgk, bgk, gn, *, tt=256):
    B, S, _ = a.shape
    nt = S // tt
    VW = GLA_HEADS * GLA_DV
    KW = GLA_HEADS * GLA_DK
    in_tile = lambda b, j: (b, jnp.where(j < nt, j, 2 * nt - 1 - j), 0)
    out_tile = lambda b, j: (b, jnp.where(j < nt, nt - 1, 2 * nt - 1 - j), 0)
    return pl.pallas_call(
        functools.partial(_gla_kernel, nt=nt, tt=tt),
        grid=(B, 2 * nt),
        in_specs=[
            pl.BlockSpec((None, tt, A_W), in_tile),
            pl.BlockSpec((2 * GLA_GATE_RANK, 2 * KW), lambda b, j: (0, 0)),
            pl.BlockSpec((1, 2 * KW), lambda b, j: (0, 0)),
            pl.BlockSpec((1, VW), lambda b, j: (0, 0)),
        ],
        out_specs=pl.BlockSpec((None, tt, VW), out_tile),
        out_shape=jax.ShapeDtypeStruct((B, S, VW), BF16),
        scratch_shapes=[pltpu.VMEM((S, VW), F32), pltpu.VMEM((VW, KW), F32)],
        compiler_params=_cparams(("arbitrary", "arbitrary")),
        name="gla",
    )(a, wgk, bgk, gn)


def _na_kernel(q_ref, k_ref, v_ref, bias_ref, o_ref, *, rows):
    r = pl.program_id(1)
    nk = NA_KROWS * GRID_W
    ks = pl.multiple_of(jnp.clip(r * NA_QROWS - NA_WIN_R // 2, 0, rows - NA_KROWS) * GRID_W, GRID_W)
    q = q_ref[...]
    k = k_ref[pl.ds(ks, nk), :]
    v = v_ref[pl.ds(ks, nk), :]
    W = NA_HEADS * NA_DH
    acc = jnp.zeros((NA_QROWS * GRID_W, W), F32)
    for h in range(NA_HEADS):
        lm = _lane_mask(W, h * NA_DH, (h + 1) * NA_DH)
        qh = jnp.where(lm, q, jnp.zeros_like(q))
        s = lax.dot_general(qh, k, NT_DIMS, preferred_element_type=F32) + bias_ref[h]
        m = jnp.max(s, axis=-1, keepdims=True)
        p = jnp.exp(s - m)
        l = jnp.sum(p, axis=-1, keepdims=True)
        oh = jnp.dot(p.astype(BF16), v, preferred_element_type=F32)
        acc = acc + jnp.where(lm, oh / l, 0.0)
    o_ref[...] = acc.astype(BF16)


def _na_bias_tables(rpb, rows):
    nr = rows // NA_QROWS
    tables = []
    for rb in (0, 1, nr - 1):
        ks = int(np.clip(rb * NA_QROWS - NA_WIN_R // 2, 0, rows - NA_KROWS))
        qr = (rb * NA_QROWS + np.arange(NA_QROWS))[:, None, None, None]
        qc = np.arange(GRID_W)[None, :, None, None]
        kr = (ks + np.arange(NA_KROWS))[None, None, :, None]
        kc = np.arange(GRID_W)[None, None, None, :]
        r0 = np.clip(qr - NA_WIN_R // 2, 0, rows - NA_WIN_R)
        c0 = np.clip(qc - NA_WIN_C // 2, 0, GRID_W - NA_WIN_C)
        ok = (kr >= r0) & (kr < r0 + NA_WIN_R) & (kc >= c0) & (kc < c0 + NA_WIN_C)
        drow = np.clip(kr - qr + NA_WIN_R - 1, 0, 2 * NA_WIN_R - 2)
        dcol = np.clip(kc - qc + NA_WIN_C - 1, 0, 2 * NA_WIN_C - 2)
        shp = (NA_QROWS, GRID_W, NA_KROWS, GRID_W)
        n = NA_QROWS * GRID_W
        m = NA_KROWS * GRID_W
        drow = np.broadcast_to(drow, shp).reshape(n, m)
        dcol = np.broadcast_to(dcol, shp).reshape(n, m)
        ok = np.broadcast_to(ok, shp).reshape(n, m)
        tables.append(jnp.where(ok[None], rpb[:, drow, dcol].astype(F32), NEG_INF))
    return jnp.stack(tables)


def _na(c, bias, *, rows):
    B, S, _ = c.shape
    nr = rows // NA_QROWS
    W = NA_HEADS * NA_DH
    nq = NA_QROWS * GRID_W
    nk = NA_KROWS * GRID_W
    variant = lambda b, r: (jnp.where(r == 0, 0, jnp.where(r == nr - 1, 2, 1)), 0, 0, 0)
    return pl.pallas_call(
        functools.partial(_na_kernel, rows=rows),
        grid=(B, nr),
        in_specs=[
            pl.BlockSpec((None, nq, W), lambda b, r: (b, r, 0)),
            pl.BlockSpec((None, S, W), lambda b, r: (b, 0, 1)),
            pl.BlockSpec((None, S, W), lambda b, r: (b, 0, 2)),
            pl.BlockSpec((None, NA_HEADS, nq, nk), variant),
        ],
        out_specs=pl.BlockSpec((None, nq, W), lambda b, r: (b, r, 0)),
        out_shape=jax.ShapeDtypeStruct((B, S, W), BF16),
        compiler_params=_cparams(("arbitrary", "arbitrary")),
        name="na",
    )(c, c, c, bias)


def _mla_kernel(q_ref, k_ref, v_ref, o_ref):
    W = MLA_HEADS * MLA_V
    acc = jnp.zeros((q_ref.shape[0], W), F32)
    for h in range(MLA_HEADS):
        sl = slice(h * MLA_HEAD_PAD, (h + 1) * MLA_HEAD_PAD)
        s = lax.dot_general(q_ref[:, sl], k_ref[:, sl], NT_DIMS, preferred_element_type=F32)
        m = jnp.max(s, axis=-1, keepdims=True)
        p = jnp.exp(s - m)
        l = jnp.sum(p, axis=-1, keepdims=True)
        oh = jnp.dot(p.astype(BF16), v_ref[...], preferred_element_type=F32)
        acc = acc + jnp.where(_lane_mask(W, h * MLA_V, (h + 1) * MLA_V), oh / l, 0.0)
    o_ref[...] = acc.astype(BF16)


def _mla(dq, dk, dv, *, tq):
    B, S, HP = dq.shape
    W = MLA_HEADS * MLA_V
    return pl.pallas_call(
        _mla_kernel,
        grid=(B, S // tq),
        in_specs=[
            pl.BlockSpec((None, tq, HP), lambda b, i: (b, i, 0)),
            pl.BlockSpec((None, S, HP), lambda b, i: (b, 0, 0)),
            pl.BlockSpec((None, S, W), lambda b, i: (b, 0, 0)),
        ],
        out_specs=pl.BlockSpec((None, tq, W), lambda b, i: (b, i, 0)),
        out_shape=jax.ShapeDtypeStruct((B, S, W), BF16),
        compiler_params=_cparams(("arbitrary", "arbitrary")),
        name="mla",
    )(dq, dk, dv)


def _out_kernel(x_ref, mod_ref, ya_ref, u_ref, yc_ref, yd_ref, wp_ref, ps_ref, wo_ref, o_ref, *, S, tm):
    i = pl.program_id(1)
    t0 = i * tm
    ne = tm + 2 * POOL_HALO
    e0 = pl.multiple_of(jnp.clip(t0 - POOL_HALO, 0, S - ne), POOL_HALO)
    ue = u_ref[pl.ds(e0, ne), :]
    ut = u_ref[pl.ds(pl.multiple_of(t0, tm), tm), :].astype(F32)
    tok = t0 + lax.broadcasted_iota(jnp.int32, (tm, ne), 0)
    src = e0 + lax.broadcasted_iota(jnp.int32, (tm, ne), 1)
    tcol = t0 + lax.broadcasted_iota(jnp.int32, (tm, 1), 0)
    mean = jnp.zeros((tm, GROUP_W), F32)
    for gi, w in enumerate(POOL_WINDOWS):
        lo = jnp.maximum(tok - w // 2, 0)
        hi = jnp.minimum(tok + w // 2, S)
        band = jnp.where(src >= lo, jnp.where(src < hi, 1.0, 0.0), 0.0).astype(BF16)
        cnt = (jnp.minimum(tcol + w // 2, S) - jnp.maximum(tcol - w // 2, 0)).astype(F32)
        sm = jnp.dot(band, ue, preferred_element_type=F32)
        mean = mean + jnp.where(_lane_mask(GROUP_W, gi * POOL_GW, (gi + 1) * POOL_GW), sm / cnt, 0.0)
    d = (mean - ut).astype(BF16)
    yb = (jnp.dot(d, wp_ref[...], preferred_element_type=F32) * ps_ref[...]).astype(BF16)
    y = jnp.concatenate([ya_ref[...], yb, yc_ref[...], yd_ref[...]], axis=1)
    t = jnp.dot(y, wo_ref[...], preferred_element_type=F32)
    o_ref[...] = x_ref[...] + (1.0 + mod_ref[5:6, :]) * t


def _out(x, mod, ya, u, yc, yd, wp, ps, wo, *, tm=256):
    B, S, D = x.shape
    tok = lambda w: pl.BlockSpec((None, tm, w), lambda b, i: (b, i, 0))
    return pl.pallas_call(
        functools.partial(_out_kernel, S=S, tm=tm),
        grid=(B, S // tm),
        in_specs=[
            tok(D),
            pl.BlockSpec((None, N_MOD, D), lambda b, i: (b, 0, 0)),
            tok(GROUP_W),
            pl.BlockSpec((None, S, GROUP_W), lambda b, i: (b, 0, 0)),
            tok(GROUP_W),
            tok(GROUP_W),
            _const_spec((GROUP_W, GROUP_W)),
            _const_spec((1, GROUP_W)),
            _const_spec((D, D)),
        ],
        out_specs=tok(D),
        out_shape=jax.ShapeDtypeStruct((B, S, D), F32),
        compiler_params=_cparams(("arbitrary", "arbitrary")),
        name="out",
    )(x, mod, ya, u, yc, yd, wp, ps, wo)


def _take_cols(w, idx):
    wz = jnp.concatenate([w, jnp.zeros((w.shape[0], 1), w.dtype)], axis=1)
    return jnp.take(wz, jnp.asarray(np.where(idx < 0, w.shape[1], idx)), axis=1)


def _win_index():
    idx = np.full((PROJ_W2,), -1, np.int64)
    idx[0:800] = np.arange(0, 800)
    idx[U_OFF:U_OFF + 256] = np.arange(800, 1056)
    idx[C_OFF:C_OFF + 768] = np.arange(1056, 1824)
    idx[DQ_OFF:DQ_OFF + 256] = np.arange(1824, 2080)
    idx[DKV_OFF:DKV_OFF + 128] = np.arange(2080, 2208)
    for h in range(MLA_HEADS):
        lo = KPE_OFF + h * MLA_HEAD_PAD + MLA_NOPE
        idx[lo:lo + MLA_ROPE] = np.arange(2208, 2240)
    return idx


def _head_pad_index(per_head, keep, src_off=0):
    idx = np.full((MLA_HEADS * MLA_HEAD_PAD,), -1, np.int64)
    for h in range(MLA_HEADS):
        idx[h * MLA_HEAD_PAD:h * MLA_HEAD_PAD + keep] = h * per_head + src_off + np.arange(keep)
    return idx


def _layer_weights(l, P):
    bf = lambda t: t.astype(BF16)
    row = lambda t: t.reshape(1, -1).astype(F32)
    w = {}
    w["nf1"], w["nmix"], w["nf2"] = row(P["norm_ffn1"][l]), row(P["norm_mix"][l]), row(P["norm_ffn2"][l])
    for n in ("ffn1_wg", "ffn1_wu", "ffn1_wd", "ffn2_wg", "ffn2_wu", "ffn2_wd", "w_out"):
        w[n] = bf(P[n][l])
    w["win"] = bf(_take_cols(P["w_in"][l], _win_index()))
    w["qn"], w["kvn"] = row(P["mla_qnorm"][l]), row(P["mla_kvnorm"][l])
    w["wuq"] = bf(_take_cols(P["mla_wuq"][l], _head_pad_index(MLA_NOPE + MLA_ROPE, MLA_NOPE + MLA_ROPE)))
    w["wuk"] = bf(_take_cols(P["mla_wukv"][l], _head_pad_index(MLA_NOPE + MLA_V, MLA_NOPE)))
    vidx = np.concatenate([h * (MLA_NOPE + MLA_V) + MLA_NOPE + np.arange(MLA_V) for h in range(MLA_HEADS)])
    w["wuv"] = bf(jnp.take(P["mla_wukv"][l], jnp.asarray(vidx), axis=1))
    KW = GLA_HEADS * GLA_DK
    wgk = jnp.zeros((2 * GLA_GATE_RANK, 2 * KW), F32)
    wgk = wgk.at[:GLA_GATE_RANK, :KW].set(P["gla_wgk_f"][l]).at[GLA_GATE_RANK:, KW:].set(P["gla_wgk_b"][l])
    w["wgk"] = bf(wgk)
    w["bgk"] = row(jnp.concatenate([P["gla_bgk_f"][l], P["gla_bgk_b"][l]]))
    w["gn"] = row(jnp.tile(P["gla_norm"][l], GLA_HEADS))
    wp = jnp.zeros((GROUP_W, GROUP_W), F32)
    for gi in range(len(POOL_WINDOWS)):
        sl = slice(gi * POOL_GW, (gi + 1) * POOL_GW)
        wp = wp.at[sl, sl].set(P["pool_w"][l][gi])
    w["wp"] = bf(wp)
    w["ps"] = row(P["pool_scale"][l])
    w["rpb"] = P["na_rpb"][l]
    return w


def _rope_tables(S):
    half = MLA_ROPE // 2
    inv = ROPE_THETA ** (-jnp.arange(half, dtype=F32) / half)
    ang = jnp.arange(S, dtype=F32)[:, None] * inv[None, :]
    cos, sin = jnp.cos(ang), jnp.sin(ang)
    ones = jnp.ones((S, MLA_NOPE), F32)
    pad = jnp.zeros((S, MLA_HEAD_PAD - MLA_NOPE - MLA_ROPE), F32)
    cos_t = jnp.concatenate([ones, cos, cos, pad], axis=1)
    sin_t = jnp.concatenate([jnp.zeros((S, MLA_NOPE), F32), -sin, sin, pad], axis=1)
    return cos_t, sin_t


def _run_stream(x, mods, W, final_w):
    B, S, D = x.shape
    rows = S // GRID_W
    cos_t, sin_t = _rope_tables(S)
    tq = 256 if S <= 2048 else 128
    for l in range(DEPTH):
        w = W[l]
        mod = mods[l]
        x = _ffn(x, mod, w["nf1"], w["ffn1_wg"], w["ffn1_wu"], w["ffn1_wd"], final_w, mi=0, final=False)
        a, u, c, dq, dk, dv = _proj(x, mod, w["nmix"], w["win"], w["qn"], w["wuq"], w["kvn"], w["wuk"],
                                    w["wuv"], cos_t, sin_t)
        ya = _gla(a, w["wgk"], w["bgk"], w["gn"])
        yc = _na(c, _na_bias_tables(w["rpb"], rows), rows=rows)
        yd = _mla(dq, dk, dv, tq=tq)
        x = _out(x, mod, ya, u, yc, yd, w["wp"], w["ps"], w["w_out"])
        x = _ffn(x, mod, w["nf2"], w["ffn2_wg"], w["ffn2_wu"], w["ffn2_wd"], final_w, mi=6,
                 final=(l == DEPTH - 1))
    return x


def kernel(x_prompt, x_sample, c_prompt, c_sample, ada_w, ada_b, norm_ffn1, ffn1_wg, ffn1_wu, ffn1_wd, norm_mix, w_in, gla_wgk_f, gla_bgk_f, gla_wgk_b, gla_bgk_b, gla_norm, pool_w, pool_scale, na_rpb, mla_qnorm, mla_wuq, mla_kvnorm, mla_wukv, w_out, norm_ffn2, ffn2_wg, ffn2_wu, ffn2_wd, final_norm):
    P = dict(norm_ffn1=norm_ffn1, ffn1_wg=ffn1_wg, ffn1_wu=ffn1_wu, ffn1_wd=ffn1_wd, norm_mix=norm_mix,
             w_in=w_in, gla_wgk_f=gla_wgk_f, gla_bgk_f=gla_bgk_f, gla_wgk_b=gla_wgk_b, gla_bgk_b=gla_bgk_b,
             gla_norm=gla_norm, pool_w=pool_w, pool_scale=pool_scale, na_rpb=na_rpb, mla_qnorm=mla_qnorm,
             mla_wuq=mla_wuq, mla_kvnorm=mla_kvnorm, mla_wukv=mla_wukv, w_out=w_out, norm_ffn2=norm_ffn2,
             ffn2_wg=ffn2_wg, ffn2_wu=ffn2_wu, ffn2_wd=ffn2_wd)
    W = [_layer_weights(l, P) for l in range(DEPTH)]
    final_w = final_norm.reshape(1, -1).astype(F32)
    bp = x_prompt.shape[0]
    c_all = jnp.concatenate([c_prompt, c_sample], axis=0)
    mod_all = _modulation(c_all, ada_w, ada_b)
    mod_all = mod_all.reshape(DEPTH, c_all.shape[0], N_MOD, D_MODEL)
    mods_p = [mod_all[l, :bp] for l in range(DEPTH)]
    mods_s = [mod_all[l, bp:] for l in range(DEPTH)]
    y_prompt = _run_stream(x_prompt, mods_p, W, final_w)
    y_sample = _run_stream(x_sample, mods_s, W, final_w)
    return (y_prompt, y_sample)
```

```python
import functools

import numpy as np
import jax
import jax.numpy as jnp
from jax import lax
from jax.experimental import pallas as pl
from jax.experimental.pallas import tpu as pltpu

F32 = jnp.float32
BF16 = jnp.bfloat16

D_MODEL = 1024
DEPTH = 2
GRID_W = 64
GROUP_W = 256
GLA_HEADS = 4
GLA_DK = 32
GLA_DV = 64
GLA_GATE_RANK = 16
GLA_GATE_NORM = 16.0
GLA_CHUNK = 64
POOL_WINDOWS = (2, 4, 8, 16)
POOL_GW = 64
POOL_HALO = 16
NA_HEADS = 4
NA_DH = 64
NA_WIN_R = 8
NA_WIN_C = 16
NA_QROWS = 4
NA_KROWS = NA_QROWS + NA_WIN_R
MLA_HEADS = 4
MLA_NOPE = 64
MLA_ROPE = 32
MLA_V = 64
MLA_Q_RANK = 256
MLA_KV_RANK = 128
MLA_HEAD_PAD = 128
MLA_SCALE = (MLA_NOPE + MLA_ROPE) ** -0.5
MLA_VT_HEAD = MLA_V + 16
MLA_VT_ROWS = MLA_HEADS * MLA_VT_HEAD
LOG2E = 1.4426950408889634
ROPE_THETA = 10000.0
D_FF = 2816
N_MOD = 9
EPS = 1e-6
NEG_INF = -1e30

A_W = 896
A_LR = 768
U_OFF = A_W
C_OFF = U_OFF + GROUP_W
DQ_OFF = C_OFF + 3 * GROUP_W
DKV_OFF = DQ_OFF + MLA_Q_RANK
KPE_OFF = DKV_OFF + MLA_KV_RANK
PROJ_W2 = KPE_OFF + MLA_HEADS * MLA_HEAD_PAD

VMEM_LIMIT = 56 * 1024 * 1024

NT_DIMS = (((1,), (1,)), ((), ()))
TN_DIMS = (((0,), (0,)), ((), ()))


def _cparams(sem):
    return pltpu.CompilerParams(dimension_semantics=sem, vmem_limit_bytes=VMEM_LIMIT)


def _silu(x):
    return x / (1.0 + jnp.exp(-x))


def _rms(x, w):
    return x * lax.rsqrt(jnp.mean(x * x, axis=-1, keepdims=True) + EPS) * w


def _modnorm(x, w, shift, scale):
    return _rms(x, w) * (1.0 + scale) + shift


def _lane_mask(width, lo, hi):
    lane = lax.broadcasted_iota(jnp.int32, (1, width), 1)
    return (lane >= lo) & (lane < hi)


def _split_dot(ones_mat, x):
    hi = x.astype(BF16)
    lo = (x - hi.astype(F32)).astype(BF16)
    return (jnp.dot(ones_mat, hi, preferred_element_type=F32)
            + jnp.dot(ones_mat, lo, preferred_element_type=F32))


def _mod_kernel(c_ref, w_ref, b_ref, o_ref):
    s = _silu(c_ref[...]).astype(BF16)
    o_ref[...] = jnp.dot(s, w_ref[...].astype(BF16), preferred_element_type=F32) + b_ref[...]


def _modulation(c_all, ada_w, ada_b):
    n = c_all.shape[0]
    L, D, W = ada_w.shape
    tn = 1152
    return pl.pallas_call(
        _mod_kernel,
        grid=(L, W // tn),
        in_specs=[
            pl.BlockSpec((n, D), lambda l, j: (0, 0)),
            pl.BlockSpec((None, D, tn), lambda l, j: (l, 0, j)),
            pl.BlockSpec((None, 1, tn), lambda l, j: (l, 0, j)),
        ],
        out_specs=pl.BlockSpec((None, n, tn), lambda l, j: (l, 0, j)),
        out_shape=jax.ShapeDtypeStruct((L, n, W), F32),
        compiler_params=_cparams(("arbitrary", "arbitrary")),
        name="mod",
    )(c_all, ada_w, ada_b.reshape(L, 1, W))


def _ffn_kernel(x_ref, mod_ref, nw_ref, wg_ref, wu_ref, wd_ref, fw_ref, o_ref, *, mi, final):
    x = x_ref[...]
    shift = mod_ref[mi:mi + 1, :]
    scale = mod_ref[mi + 1:mi + 2, :]
    gate = mod_ref[mi + 2:mi + 3, :]
    h = _modnorm(x, nw_ref[...], shift, scale).astype(BF16)
    g = jnp.dot(h, wg_ref[...], preferred_element_type=F32)
    u = jnp.dot(h, wu_ref[...], preferred_element_type=F32)
    a = (_silu(g) * u).astype(BF16)
    y = jnp.dot(a, wd_ref[...], preferred_element_type=F32)
    out = x + (0.5 * (1.0 + gate)) * y
    if final:
        out = _rms(out, fw_ref[...])
    o_ref[...] = out


def _const_spec(shape):
    zeros = (0,) * len(shape)
    return pl.BlockSpec(shape, lambda b, i: zeros, pipeline_mode=pl.Buffered(1))


def _ffn(x, mod, nw, wg, wu, wd, fw, *, mi, final, tm=512):
    B, S, D = x.shape
    F = wg.shape[1]
    return pl.pallas_call(
        functools.partial(_ffn_kernel, mi=mi, final=final),
        grid=(B, S // tm),
        in_specs=[
            pl.BlockSpec((None, tm, D), lambda b, i: (b, i, 0)),
            pl.BlockSpec((None, N_MOD, D), lambda b, i: (b, 0, 0)),
            _const_spec((1, D)),
            _const_spec((D, F)),
            _const_spec((D, F)),
            _const_spec((F, D)),
            _const_spec((1, D)),
        ],
        out_specs=pl.BlockSpec((None, tm, D), lambda b, i: (b, i, 0)),
        out_shape=jax.ShapeDtypeStruct((B, S, D), F32),
        compiler_params=_cparams(("arbitrary", "arbitrary")),
        name="ffn",
    )(x, mod, nw, wg, wu, wd, fw)


def _rope(x, cos, sins):
    n = x.shape[1]
    lane = lax.broadcasted_iota(jnp.int32, (1, n), 1) % MLA_HEAD_PAD
    first_half = (lane >= MLA_NOPE) & (lane < MLA_NOPE + MLA_ROPE // 2)
    partner = jnp.where(first_half, pltpu.roll(x, n - MLA_ROPE // 2, 1), pltpu.roll(x, MLA_ROPE // 2, 1))
    return x * cos + partner * sins


def _proj_kernel(x_ref, mod_ref, nw_ref, win_ref, qn_ref, wuq_ref, kvn_ref, wuk_ref, wuvt_ref, vone_ref,
                 cos_ref, sin_ref, a_ref, u_ref, c_ref, dq_ref, dk_ref, dvt_ref):
    x = x_ref[...]
    h = _modnorm(x, nw_ref[...], mod_ref[3:4, :], mod_ref[4:5, :]).astype(BF16)
    p = jnp.dot(h, win_ref[...], preferred_element_type=F32)
    a_ref[...] = p[:, 0:A_W].astype(BF16)
    u_ref[...] = p[:, U_OFF:U_OFF + GROUP_W].astype(BF16)
    c_ref[:, 0:GROUP_W] = (p[:, C_OFF:C_OFF + GROUP_W] * (NA_DH ** -0.5)).astype(BF16)
    c_ref[:, GROUP_W:3 * GROUP_W] = p[:, C_OFF + GROUP_W:C_OFF + 3 * GROUP_W].astype(BF16)
    qn = _rms(p[:, DQ_OFF:DQ_OFF + MLA_Q_RANK], qn_ref[...]).astype(BF16)
    kvn = _rms(p[:, DKV_OFF:DKV_OFF + MLA_KV_RANK], kvn_ref[...]).astype(BF16)
    q = jnp.dot(qn, wuq_ref[...], preferred_element_type=F32)
    k = jnp.dot(kvn, wuk_ref[...], preferred_element_type=F32) + p[:, KPE_OFF:PROJ_W2]
    vt = lax.dot_general(wuvt_ref[...], kvn, NT_DIMS, preferred_element_type=F32) + vone_ref[...]
    cos = jnp.concatenate([cos_ref[...]] * MLA_HEADS, axis=1)
    sins = jnp.concatenate([sin_ref[...]] * MLA_HEADS, axis=1)
    dq_ref[...] = (_rope(q, cos, sins) * (MLA_SCALE * LOG2E)).astype(BF16)
    dk_ref[...] = _rope(k, cos, sins).astype(BF16)
    dvt_ref[...] = vt.astype(BF16)


def _proj(x, mod, nw, win, qn, wuq, kvn, wuk, wuvt, vone, cos, sins, *, tm=512):
    B, S, D = x.shape
    HP = MLA_HEADS * MLA_HEAD_PAD
    tok = lambda w: pl.BlockSpec((None, tm, w), lambda b, i: (b, i, 0))
    shapes = [A_W, GROUP_W, 3 * GROUP_W, HP, HP]
    return pl.pallas_call(
        _proj_kernel,
        grid=(B, S // tm),
        in_specs=[
            tok(D),
            pl.BlockSpec((None, N_MOD, D), lambda b, i: (b, 0, 0)),
            _const_spec((1, D)),
            _const_spec((D, PROJ_W2)),
            _const_spec((1, MLA_Q_RANK)),
            _const_spec((MLA_Q_RANK, HP)),
            _const_spec((1, MLA_KV_RANK)),
            _const_spec((MLA_KV_RANK, HP)),
            _const_spec((MLA_VT_ROWS, MLA_KV_RANK)),
            _const_spec((MLA_VT_ROWS, 1)),
            pl.BlockSpec((tm, MLA_HEAD_PAD), lambda b, i: (i, 0)),
            pl.BlockSpec((tm, MLA_HEAD_PAD), lambda b, i: (i, 0)),
        ],
        out_specs=[tok(w) for w in shapes] + [pl.BlockSpec((None, MLA_VT_ROWS, tm), lambda b, i: (b, 0, i))],
        out_shape=[jax.ShapeDtypeStruct((B, S, w), BF16) for w in shapes]
        + [jax.ShapeDtypeStruct((B, MLA_VT_ROWS, S), BF16)],
        compiler_params=_cparams(("arbitrary", "arbitrary")),
        name="proj",
    )(x, mod, nw, win, qn, wuq, kvn, wuk, wuvt, vone, cos, sins)


def _log_sigmoid(z):
    return -(jnp.maximum(-z, 0.0) + jnp.log(1.0 + jnp.exp(-jnp.abs(z))))


def _gla_kernel(a_ref, wgk_ref, bgk_ref, gn_ref, o_ref, ofwd_ref, st_ref, *, nt, tt):
    j = pl.program_id(1)
    C = GLA_CHUNK
    nc = tt // C
    KW = GLA_HEADS * GLA_DK
    VW = GLA_HEADS * GLA_DV
    qscale = GLA_DK ** -0.5

    @pl.when((j == 0) | (j == nt))
    def _():
        st_ref[...] = jnp.zeros_like(st_ref)

    row = lax.broadcasted_iota(jnp.int32, (tt, tt), 0)
    col = lax.broadcasted_iota(jnp.int32, (tt, tt), 1)
    cum_mat = jnp.where((row >= col) & (row // C == col // C), 1.0, 0.0).astype(BF16)
    srow = lax.broadcasted_iota(jnp.int32, (GLA_HEADS * C, C), 0) % C
    scol = lax.broadcasted_iota(jnp.int32, (GLA_HEADS * C, C), 1)
    st_mask = (lax.broadcasted_iota(jnp.int32, (VW, KW), 0) // GLA_DV
               == lax.broadcasted_iota(jnp.int32, (VW, KW), 1) // GLA_DK)
    kmasks = [_lane_mask(KW, h * GLA_DK, (h + 1) * GLA_DK) for h in range(GLA_HEADS)]
    vmasks = [_lane_mask(VW, h * GLA_DV, (h + 1) * GLA_DV) for h in range(GLA_HEADS)]

    lr = a_ref[:, A_LR:A_LR + 2 * GLA_GATE_RANK]

    def log_decay(lo):
        z = jnp.dot(lr, wgk_ref[:, lo:lo + KW], preferred_element_type=F32) + bgk_ref[:, lo:lo + KW]
        return _log_sigmoid(z) * (1.0 / GLA_GATE_NORM)

    def chunk_out(q_intra, k_intra, q_inter, k_state, decay, vc, keep):
        qs = jnp.concatenate([jnp.where(m, q_intra, 0.0) for m in kmasks], axis=0).astype(BF16)
        att = lax.dot_general(qs, k_intra.astype(BF16), NT_DIMS, preferred_element_type=F32)
        att = jnp.where(keep, att, 0.0).astype(BF16)
        full = jnp.dot(att, vc, preferred_element_type=F32)
        o = jnp.zeros((C, VW), F32)
        for h in range(GLA_HEADS):
            o = o + jnp.where(vmasks[h], full[h * C:(h + 1) * C, :], 0.0)
        st = st_ref[...]
        o = o + lax.dot_general(q_inter.astype(BF16), st.astype(BF16), NT_DIMS,
                                preferred_element_type=F32)
        kv = lax.dot_general(vc, k_state.astype(BF16), TN_DIMS, preferred_element_type=F32)
        st_ref[...] = st * decay + jnp.where(st_mask, kv, 0.0)
        return o

    @pl.when(j < nt)
    def _():
        g = log_decay(0)
        b = _split_dot(cum_mat, g)
        for c in range(nc):
            r0 = c * C
            bc = b[r0:r0 + C, :]
            bl = b[r0 + C - 1:r0 + C, :]
            qc = a_ref[r0:r0 + C, 0:KW].astype(F32)
            kc = a_ref[r0:r0 + C, KW:2 * KW].astype(F32)
            vc = a_ref[r0:r0 + C, 2 * KW:2 * KW + VW]
            qd = qc * (jnp.exp(bc) * qscale)
            o = chunk_out(qd, kc * jnp.exp(-bc), qd, kc * jnp.exp(bl - bc), jnp.exp(bl), vc,
                          srow >= scol)
            ofwd_ref[pl.ds(pl.multiple_of(j * tt + r0, C), C), :] = o

    @pl.when(j >= nt)
    def _():
        t = 2 * nt - 1 - j
        g = log_decay(KW)
        binc = _split_dot(cum_mat, g)
        e = binc - g
        outs = [None] * nc
        for c in reversed(range(nc)):
            r0 = c * C
            ec = e[r0:r0 + C, :]
            gt = binc[r0 + C - 1:r0 + C, :]
            qc = a_ref[r0:r0 + C, 0:KW].astype(F32)
            kc = a_ref[r0:r0 + C, KW:2 * KW].astype(F32)
            vc = a_ref[r0:r0 + C, 2 * KW:2 * KW + VW]
            ke = kc * jnp.exp(ec)
            o = chunk_out(qc * (jnp.exp(-ec) * qscale), ke, qc * (jnp.exp(gt - ec) * qscale), ke,
                          jnp.exp(gt), vc, srow < scol)
            outs[c] = o + ofwd_ref[pl.ds(pl.multiple_of(t * tt + r0, C), C), :]
        o = jnp.concatenate(outs, axis=0)
        seg = jnp.where(lax.broadcasted_iota(jnp.int32, (VW, VW), 0) // GLA_DV
                        == lax.broadcasted_iota(jnp.int32, (VW, VW), 1) // GLA_DV, 1.0, 0.0).astype(BF16)
        ms = _split_dot_rhs(o * o, seg) * (1.0 / GLA_DV)
        gout = a_ref[:, 2 * KW + VW:2 * KW + 2 * VW].astype(F32)
        o_ref[...] = (o * lax.rsqrt(ms + EPS) * gn_ref[...] * _silu(gout)).astype(BF16)


def _split_dot_rhs(x, ones_mat):
    hi = x.astype(BF16)
    lo = (x - hi.astype(F32)).astype(BF16)
    return (jnp.dot(hi, ones_mat, preferred_element_type=F32)
            + jnp.dot(lo, ones_mat, preferred_element_type=F32))


def _gla(a, wgk, bgk, gn, *, tt=256):
    B, S, _ = a.shape
    nt = S // tt
    VW = GLA_HEADS * GLA_DV
    KW = GLA_HEADS * GLA_DK
    in_tile = lambda b, j: (b, jnp.where(j < nt, j, 2 * nt - 1 - j), 0)
    out_tile = lambda b, j: (b, jnp.where(j < nt, nt - 1, 2 * nt - 1 - j), 0)
    return pl.pallas_call(
        functools.partial(_gla_kernel, nt=nt, tt=tt),
        grid=(B, 2 * nt),
        in_specs=[
            pl.BlockSpec((None, tt, A_W), in_tile),
            pl.BlockSpec((2 * GLA_GATE_RANK, 2 * KW), lambda b, j: (0, 0)),
            pl.BlockSpec((1, 2 * KW), lambda b, j: (0, 0)),
            pl.BlockSpec((1, VW), lambda b, j: (0, 0)),
        ],
        out_specs=pl.BlockSpec((None, tt, VW), out_tile),
        out_shape=jax.ShapeDtypeStruct((B, S, VW), BF16),
        scratch_shapes=[pltpu.VMEM((S, VW), F32), pltpu.VMEM((VW, KW), F32)],
        compiler_params=_cparams(("arbitrary", "arbitrary")),
        name="gla",
    )(a, wgk, bgk, gn)


def _na_kernel(q_ref, k_ref, v_ref, bias_ref, o_ref, *, rows):
    r = pl.program_id(1)
    nk = NA_KROWS * GRID_W
    ks = pl.multiple_of(jnp.clip(r * NA_QROWS - NA_WIN_R // 2, 0, rows - NA_KROWS) * GRID_W, GRID_W)
    q = q_ref[...]
    k = k_ref[pl.ds(ks, nk), :]
    v = v_ref[pl.ds(ks, nk), :]
    W = NA_HEADS * NA_DH
    acc = jnp.zeros((NA_QROWS * GRID_W, W), F32)
    for h in range(NA_HEADS):
        lm = _lane_mask(W, h * NA_DH, (h + 1) * NA_DH)
        qh = jnp.where(lm, q, jnp.zeros_like(q))
        s = lax.dot_general(qh, k, NT_DIMS, preferred_element_type=F32) + bias_ref[h]
        m = jnp.max(s, axis=-1, keepdims=True)
        p = jnp.exp(s - m)
        l = jnp.sum(p, axis=-1, keepdims=True)
        oh = jnp.dot(p.astype(BF16), v, preferred_element_type=F32)
        acc = acc + jnp.where(lm, oh / l, 0.0)
    o_ref[...] = acc.astype(BF16)


def _na_bias_tables(rpb, rows):
    nr = rows // NA_QROWS
    H, ndr, ndc = rpb.shape
    period = 2 * GRID_W
    gap = jnp.full((H, ndr, period - ndc), NEG_INF, F32)
    wpad = jnp.concatenate([rpb[..., NA_WIN_C - 1:].astype(F32), gap, rpb[..., :NA_WIN_C - 1].astype(F32)], -1)
    skew = jnp.tile(wpad, (1, 1, GRID_W))[..., :GRID_W * (period - 1)]
    blocks = skew.reshape(H, ndr, GRID_W, period - 1)[..., :GRID_W]
    qc = np.arange(GRID_W)[:, None]
    kc = np.arange(GRID_W)[None, :]
    c0 = np.clip(qc - NA_WIN_C // 2, 0, GRID_W - NA_WIN_C)
    col_ok = (kc >= c0) & (kc < c0 + NA_WIN_C)
    blocks = jnp.where(col_ok[None, None], blocks, NEG_INF)
    masked = jnp.full((H, GRID_W, GRID_W), NEG_INF, F32)
    tables = []
    for rb in (0, 1, nr - 1):
        ks = int(np.clip(rb * NA_QROWS - NA_WIN_R // 2, 0, rows - NA_KROWS))
        qrows = []
        for i in range(NA_QROWS):
            qr = rb * NA_QROWS + i
            r0 = int(np.clip(qr - NA_WIN_R // 2, 0, rows - NA_WIN_R))
            parts = []
            for jj in range(NA_KROWS):
                kr = ks + jj
                parts.append(blocks[:, kr - qr + NA_WIN_R - 1] if r0 <= kr < r0 + NA_WIN_R else masked)
            qrows.append(jnp.concatenate(parts, axis=-1))
        tables.append(jnp.concatenate(qrows, axis=-2))
    return jnp.stack(tables)


def _na(c, bias, *, rows):
    B, S, _ = c.shape
    nr = rows // NA_QROWS
    W = NA_HEADS * NA_DH
    nq = NA_QROWS * GRID_W
    nk = NA_KROWS * GRID_W
    variant = lambda b, r: (jnp.where(r == 0, 0, jnp.where(r == nr - 1, 2, 1)), 0, 0, 0)
    return pl.pallas_call(
        functools.partial(_na_kernel, rows=rows),
        grid=(B, nr),
        in_specs=[
            pl.BlockSpec((None, nq, W), lambda b, r: (b, r, 0)),
            pl.BlockSpec((None, S, W), lambda b, r: (b, 0, 1)),
            pl.BlockSpec((None, S, W), lambda b, r: (b, 0, 2)),
            pl.BlockSpec((None, NA_HEADS, nq, nk), variant),
        ],
        out_specs=pl.BlockSpec((None, nq, W), lambda b, r: (b, r, 0)),
        out_shape=jax.ShapeDtypeStruct((B, S, W), BF16),
        compiler_params=_cparams(("arbitrary", "arbitrary")),
        name="na",
    )(c, c, c, bias)


def _mla_kernel(q_ref, k_ref, vt_ref, o_ref, *, tk):
    S = k_ref.shape[0]
    items = [(h, c0) for h in range(MLA_HEADS) for c0 in range(0, S, tk)]

    def scores(h, c0):
        sl = slice(h * MLA_HEAD_PAD, (h + 1) * MLA_HEAD_PAD)
        return lax.dot_general(k_ref[c0:c0 + tk, sl], q_ref[:, sl], NT_DIMS, preferred_element_type=F32)

    st_next = scores(*items[0])
    m = acc = None
    for n, (h, c0) in enumerate(items):
        st = st_next
        if n + 1 < len(items):
            st_next = scores(*items[n + 1])
        mc = jnp.max(st, axis=0, keepdims=True)
        m_new = mc if c0 == 0 else jnp.maximum(m, mc)
        p = jnp.exp2(st - m_new).astype(BF16)
        pv = jnp.dot(vt_ref[h * MLA_VT_HEAD:(h + 1) * MLA_VT_HEAD, c0:c0 + tk], p, preferred_element_type=F32)
        acc = pv if c0 == 0 else acc * jnp.exp2(m - m_new) + pv
        m = m_new
        if c0 + tk == S:
            o_ref[h * MLA_V:(h + 1) * MLA_V, :] = (acc[0:MLA_V, :] / acc[MLA_V:MLA_V + 1, :]).astype(BF16)


def _mla(dq, dk, dvt, *, tq=256, tk=2048):
    B, S, HP = dq.shape
    W = MLA_HEADS * MLA_V
    return pl.pallas_call(
        functools.partial(_mla_kernel, tk=min(tk, S)),
        grid=(B, S // tq),
        in_specs=[
            pl.BlockSpec((None, tq, HP), lambda b, i: (b, i, 0)),
            pl.BlockSpec((None, S, HP), lambda b, i: (b, 0, 0)),
            pl.BlockSpec((None, MLA_VT_ROWS, S), lambda b, i: (b, 0, 0)),
        ],
        out_specs=pl.BlockSpec((None, W, tq), lambda b, i: (b, 0, i)),
        out_shape=jax.ShapeDtypeStruct((B, W, S), BF16),
        compiler_params=_cparams(("arbitrary", "arbitrary")),
        name="mla",
    )(dq, dk, dvt)


def _out_kernel(x_ref, mod_ref, ya_ref, u_ref, yc_ref, ydt_ref, wp_ref, ps_ref, wo_ref, o_ref, *, S, tm):
    i = pl.program_id(1)
    t0 = i * tm
    ne = tm + 2 * POOL_HALO
    e0 = pl.multiple_of(jnp.clip(t0 - POOL_HALO, 0, S - ne), POOL_HALO)
    ue = u_ref[pl.ds(e0, ne), :]
    ut = u_ref[pl.ds(pl.multiple_of(t0, tm), tm), :].astype(F32)
    tok = t0 + lax.broadcasted_iota(jnp.int32, (tm, ne), 0)
    src = e0 + lax.broadcasted_iota(jnp.int32, (tm, ne), 1)
    tcol = t0 + lax.broadcasted_iota(jnp.int32, (tm, 1), 0)
    mean = jnp.zeros((tm, GROUP_W), F32)
    for gi, w in enumerate(POOL_WINDOWS):
        lo = jnp.maximum(tok - w // 2, 0)
        hi = jnp.minimum(tok + w // 2, S)
        band = jnp.where(src >= lo, jnp.where(src < hi, 1.0, 0.0), 0.0).astype(BF16)
        cnt = (jnp.minimum(tcol + w // 2, S) - jnp.maximum(tcol - w // 2, 0)).astype(F32)
        sm = jnp.dot(band, ue, preferred_element_type=F32)
        mean = mean + jnp.where(_lane_mask(GROUP_W, gi * POOL_GW, (gi + 1) * POOL_GW), sm / cnt, 0.0)
    d = (mean - ut).astype(BF16)
    yb = (jnp.dot(d, wp_ref[...], preferred_element_type=F32) * ps_ref[...]).astype(BF16)
    y = jnp.concatenate([ya_ref[...], yb, yc_ref[...]], axis=1)
    t = jnp.dot(y, wo_ref[0:3 * GROUP_W, :], preferred_element_type=F32)
    t = t + lax.dot_general(ydt_ref[...], wo_ref[3 * GROUP_W:4 * GROUP_W, :], TN_DIMS,
                            preferred_element_type=F32)
    o_ref[...] = x_ref[...] + (1.0 + mod_ref[5:6, :]) * t


def _out(x, mod, ya, u, yc, ydt, wp, ps, wo, *, tm=256):
    B, S, D = x.shape
    tok = lambda w: pl.BlockSpec((None, tm, w), lambda b, i: (b, i, 0))
    return pl.pallas_call(
        functools.partial(_out_kernel, S=S, tm=tm),
        grid=(B, S // tm),
        in_specs=[
            tok(D),
            pl.BlockSpec((None, N_MOD, D), lambda b, i: (b, 0, 0)),
            tok(GROUP_W),
            pl.BlockSpec((None, S, GROUP_W), lambda b, i: (b, 0, 0)),
            tok(GROUP_W),
            pl.BlockSpec((None, GROUP_W, tm), lambda b, i: (b, 0, i)),
            _const_spec((GROUP_W, GROUP_W)),
            _const_spec((1, GROUP_W)),
            _const_spec((D, D)),
        ],
        out_specs=tok(D),
        out_shape=jax.ShapeDtypeStruct((B, S, D), F32),
        compiler_params=_cparams(("arbitrary", "arbitrary")),
        name="out",
    )(x, mod, ya, u, yc, ydt, wp, ps, wo)


def _take_cols(w, idx):
    parts = []
    start = 0
    for end in range(1, len(idx) + 1):
        same_run = end < len(idx) and (
            (idx[end] < 0 and idx[end - 1] < 0) or (idx[end - 1] >= 0 and idx[end] == idx[end - 1] + 1))
        if not same_run:
            if idx[start] < 0:
                parts.append(jnp.zeros((w.shape[0], end - start), w.dtype))
            else:
                parts.append(w[:, int(idx[start]):int(idx[start]) + end - start])
            start = end
    return jnp.concatenate(parts, axis=1)


def _win_index():
    idx = np.full((PROJ_W2,), -1, np.int64)
    idx[0:800] = np.arange(0, 800)
    idx[U_OFF:U_OFF + 256] = np.arange(800, 1056)
    idx[C_OFF:C_OFF + 768] = np.arange(1056, 1824)
    idx[DQ_OFF:DQ_OFF + 256] = np.arange(1824, 2080)
    idx[DKV_OFF:DKV_OFF + 128] = np.arange(2080, 2208)
    for h in range(MLA_HEADS):
        lo = KPE_OFF + h * MLA_HEAD_PAD + MLA_NOPE
        idx[lo:lo + MLA_ROPE] = np.arange(2208, 2240)
    return idx


def _head_pad_index(per_head, keep, src_off=0):
    idx = np.full((MLA_HEADS * MLA_HEAD_PAD,), -1, np.int64)
    for h in range(MLA_HEADS):
        idx[h * MLA_HEAD_PAD:h * MLA_HEAD_PAD + keep] = h * per_head + src_off + np.arange(keep)
    return idx


def _layer_weights(l, P):
    bf = lambda t: t.astype(BF16)
    row = lambda t: t.reshape(1, -1).astype(F32)
    w = {}
    w["nf1"], w["nmix"], w["nf2"] = row(P["norm_ffn1"][l]), row(P["norm_mix"][l]), row(P["norm_ffn2"][l])
    for n in ("ffn1_wg", "ffn1_wu", "ffn1_wd", "ffn2_wg", "ffn2_wu", "ffn2_wd", "w_out"):
        w[n] = bf(P[n][l])
    w["win"] = bf(_take_cols(P["w_in"][l], _win_index()))
    w["qn"], w["kvn"] = row(P["mla_qnorm"][l]), row(P["mla_kvnorm"][l])
    w["wuq"] = bf(_take_cols(P["mla_wuq"][l], _head_pad_index(MLA_NOPE + MLA_ROPE, MLA_NOPE + MLA_ROPE)))
    w["wuk"] = bf(_take_cols(P["mla_wukv"][l], _head_pad_index(MLA_NOPE + MLA_V, MLA_NOPE)))
    vidx = np.full((MLA_VT_ROWS,), -1, np.int64)
    vone = np.zeros((MLA_VT_ROWS, 1), np.float32)
    for h in range(MLA_HEADS):
        vidx[h * MLA_VT_HEAD:h * MLA_VT_HEAD + MLA_V] = h * (MLA_NOPE + MLA_V) + MLA_NOPE + np.arange(MLA_V)
        vone[h * MLA_VT_HEAD + MLA_V:(h + 1) * MLA_VT_HEAD] = 1.0
    w["wuvt"] = bf(_take_cols(P["mla_wukv"][l], vidx).T)
    w["vone"] = jnp.asarray(vone)
    KW = GLA_HEADS * GLA_DK
    wgk = jnp.zeros((2 * GLA_GATE_RANK, 2 * KW), F32)
    wgk = wgk.at[:GLA_GATE_RANK, :KW].set(P["gla_wgk_f"][l]).at[GLA_GATE_RANK:, KW:].set(P["gla_wgk_b"][l])
    w["wgk"] = bf(wgk)
    w["bgk"] = row(jnp.concatenate([P["gla_bgk_f"][l], P["gla_bgk_b"][l]]))
    w["gn"] = row(jnp.tile(P["gla_norm"][l], GLA_HEADS))
    wp = jnp.zeros((GROUP_W, GROUP_W), F32)
    for gi in range(len(POOL_WINDOWS)):
        sl = slice(gi * POOL_GW, (gi + 1) * POOL_GW)
        wp = wp.at[sl, sl].set(P["pool_w"][l][gi])
    w["wp"] = bf(wp)
    w["ps"] = row(P["pool_scale"][l])
    w["rpb"] = P["na_rpb"][l]
    return w


def _rope_tables(S):
    half = MLA_ROPE // 2
    inv = ROPE_THETA ** (-jnp.arange(half, dtype=F32) / half)
    ang = jnp.arange(S, dtype=F32)[:, None] * inv[None, :]
    cos, sin = jnp.cos(ang), jnp.sin(ang)
    ones = jnp.ones((S, MLA_NOPE), F32)
    pad = jnp.zeros((S, MLA_HEAD_PAD - MLA_NOPE - MLA_ROPE), F32)
    cos_t = jnp.concatenate([ones, cos, cos, pad], axis=1)
    sin_t = jnp.concatenate([jnp.zeros((S, MLA_NOPE), F32), -sin, sin, pad], axis=1)
    return cos_t, sin_t


def _run_stream(x, mods, W, final_w):
    B, S, D = x.shape
    rows = S // GRID_W
    cos_t, sin_t = _rope_tables(S)
    for l in range(DEPTH):
        w = W[l]
        mod = mods[l]
        x = _ffn(x, mod, w["nf1"], w["ffn1_wg"], w["ffn1_wu"], w["ffn1_wd"], final_w, mi=0, final=False)
        a, u, c, dq, dk, dvt = _proj(x, mod, w["nmix"], w["win"], w["qn"], w["wuq"], w["kvn"], w["wuk"],
                                     w["wuvt"], w["vone"], cos_t, sin_t)
        ya = _gla(a, w["wgk"], w["bgk"], w["gn"])
        yc = _na(c, _na_bias_tables(w["rpb"], rows), rows=rows)
        ydt = _mla(dq, dk, dvt)
        x = _out(x, mod, ya, u, yc, ydt, w["wp"], w["ps"], w["w_out"])
        x = _ffn(x, mod, w["nf2"], w["ffn2_wg"], w["ffn2_wu"], w["ffn2_wd"], final_w, mi=6,
                 final=(l == DEPTH - 1))
    return x


def kernel(x_prompt, x_sample, c_prompt, c_sample, ada_w, ada_b, norm_ffn1, ffn1_wg, ffn1_wu, ffn1_wd, norm_mix, w_in, gla_wgk_f, gla_bgk_f, gla_wgk_b, gla_bgk_b, gla_norm, pool_w, pool_scale, na_rpb, mla_qnorm, mla_wuq, mla_kvnorm, mla_wukv, w_out, norm_ffn2, ffn2_wg, ffn2_wu, ffn2_wd, final_norm):
    P = dict(norm_ffn1=norm_ffn1, ffn1_wg=ffn1_wg, ffn1_wu=ffn1_wu, ffn1_wd=ffn1_wd, norm_mix=norm_mix,
             w_in=w_in, gla_wgk_f=gla_wgk_f, gla_bgk_f=gla_bgk_f, gla_wgk_b=gla_wgk_b, gla_bgk_b=gla_bgk_b,
             gla_norm=gla_norm, pool_w=pool_w, pool_scale=pool_scale, na_rpb=na_rpb, mla_qnorm=mla_qnorm,
             mla_wuq=mla_wuq, mla_kvnorm=mla_kvnorm, mla_wukv=mla_wukv, w_out=w_out, norm_ffn2=norm_ffn2,
             ffn2_wg=ffn2_wg, ffn2_wu=ffn2_wu, ffn2_wd=ffn2_wd)
    W = [_layer_weights(l, P) for l in range(DEPTH)]
    final_w = final_norm.reshape(1, -1).astype(F32)
    bp = x_prompt.shape[0]
    c_all = jnp.concatenate([c_prompt, c_sample], axis=0)
    mod_all = _modulation(c_all, ada_w, ada_b)
    mod_all = mod_all.reshape(DEPTH, c_all.shape[0], N_MOD, D_MODEL)
    mods_p = [mod_all[l, :bp] for l in range(DEPTH)]
    mods_s = [mod_all[l, bp:] for l in range(DEPTH)]
    y_prompt = _run_stream(x_prompt, mods_p, W, final_w)
    y_sample = _run_stream(x_sample, mods_s, W, final_w)
    return (y_prompt, y_sample)
```

```python
import functools

import numpy as np
import jax
import jax.numpy as jnp
from jax import lax
from jax.experimental import pallas as pl
from jax.experimental.pallas import tpu as pltpu

F32 = jnp.float32
BF16 = jnp.bfloat16

D_MODEL = 1024
DEPTH = 2
GRID_W = 64
GROUP_W = 256
GLA_HEADS = 4
GLA_DK = 32
GLA_DV = 64
GLA_GATE_RANK = 16
GLA_GATE_NORM = 16.0
GLA_CHUNK = 64
POOL_WINDOWS = (2, 4, 8, 16)
POOL_GW = 64
POOL_HALO = 16
NA_HEADS = 4
NA_DH = 64
NA_WIN_R = 8
NA_WIN_C = 16
NA_QROWS = 4
NA_KROWS = NA_QROWS + NA_WIN_R
MLA_HEADS = 4
MLA_NOPE = 64
MLA_ROPE = 32
MLA_V = 64
MLA_Q_RANK = 256
MLA_KV_RANK = 128
MLA_HEAD_PAD = 128
MLA_SCALE = (MLA_NOPE + MLA_ROPE) ** -0.5
MLA_VT_HEAD = MLA_V + 16
MLA_VT_ROWS = MLA_HEADS * MLA_VT_HEAD
LOG2E = 1.4426950408889634
ROPE_THETA = 10000.0
D_FF = 2816
N_MOD = 9
EPS = 1e-6
NEG_INF = -1e30

A_W = 896
A_LR = 768
U_OFF = A_W
C_OFF = U_OFF + GROUP_W
DQ_OFF = C_OFF + 3 * GROUP_W
DKV_OFF = DQ_OFF + MLA_Q_RANK
KPE_OFF = DKV_OFF + MLA_KV_RANK
PROJ_W2 = KPE_OFF + MLA_HEADS * MLA_HEAD_PAD

VMEM_LIMIT = 56 * 1024 * 1024

NT_DIMS = (((1,), (1,)), ((), ()))
TN_DIMS = (((0,), (0,)), ((), ()))


def _cparams(sem):
    return pltpu.CompilerParams(dimension_semantics=sem, vmem_limit_bytes=VMEM_LIMIT)


def _silu(x):
    return x / (1.0 + jnp.exp(-x))


def _rms(x, w):
    return x * lax.rsqrt(jnp.mean(x * x, axis=-1, keepdims=True) + EPS) * w


def _modnorm(x, w, shift, scale):
    return _rms(x, w) * (1.0 + scale) + shift


def _lane_mask(width, lo, hi):
    lane = lax.broadcasted_iota(jnp.int32, (1, width), 1)
    return (lane >= lo) & (lane < hi)


def _split_dot(ones_mat, x):
    hi = x.astype(BF16)
    lo = (x - hi.astype(F32)).astype(BF16)
    return (jnp.dot(ones_mat, hi, preferred_element_type=F32)
            + jnp.dot(ones_mat, lo, preferred_element_type=F32))


def _mod_kernel(c_ref, w_ref, b_ref, o_ref):
    s = _silu(c_ref[...]).astype(BF16)
    o_ref[...] = jnp.dot(s, w_ref[...].astype(BF16), preferred_element_type=F32) + b_ref[...]


def _modulation(c_all, ada_w, ada_b):
    n = c_all.shape[0]
    L, D, W = ada_w.shape
    tn = 1152
    return pl.pallas_call(
        _mod_kernel,
        grid=(L, W // tn),
        in_specs=[
            pl.BlockSpec((n, D), lambda l, j: (0, 0)),
            pl.BlockSpec((None, D, tn), lambda l, j: (l, 0, j)),
            pl.BlockSpec((None, 1, tn), lambda l, j: (l, 0, j)),
        ],
        out_specs=pl.BlockSpec((None, n, tn), lambda l, j: (l, 0, j)),
        out_shape=jax.ShapeDtypeStruct((L, n, W), F32),
        compiler_params=_cparams(("arbitrary", "arbitrary")),
        name="mod",
    )(c_all, ada_w, ada_b.reshape(L, 1, W))


def _ffn_kernel(x_ref, mod_ref, nw_ref, wg_ref, wu_ref, wd_ref, fw_ref, o_ref, *, mi, final):
    x = x_ref[...]
    shift = mod_ref[mi:mi + 1, :]
    scale = mod_ref[mi + 1:mi + 2, :]
    gate = mod_ref[mi + 2:mi + 3, :]
    h = _modnorm(x, nw_ref[...], shift, scale).astype(BF16)
    g = jnp.dot(h, wg_ref[...], preferred_element_type=F32)
    u = jnp.dot(h, wu_ref[...], preferred_element_type=F32)
    a = (_silu(g) * u).astype(BF16)
    y = jnp.dot(a, wd_ref[...], preferred_element_type=F32)
    out = x + (0.5 * (1.0 + gate)) * y
    if final:
        out = _rms(out, fw_ref[...])
    o_ref[...] = out


def _const_spec(shape):
    zeros = (0,) * len(shape)
    return pl.BlockSpec(shape, lambda b, i: zeros, pipeline_mode=pl.Buffered(1))


def _ffn(x, mod, nw, wg, wu, wd, fw, *, mi, final, tm=512):
    B, S, D = x.shape
    F = wg.shape[1]
    return pl.pallas_call(
        functools.partial(_ffn_kernel, mi=mi, final=final),
        grid=(B, S // tm),
        in_specs=[
            pl.BlockSpec((None, tm, D), lambda b, i: (b, i, 0)),
            pl.BlockSpec((None, N_MOD, D), lambda b, i: (b, 0, 0)),
            _const_spec((1, D)),
            _const_spec((D, F)),
            _const_spec((D, F)),
            _const_spec((F, D)),
            _const_spec((1, D)),
        ],
        out_specs=pl.BlockSpec((None, tm, D), lambda b, i: (b, i, 0)),
        out_shape=jax.ShapeDtypeStruct((B, S, D), F32),
        compiler_params=_cparams(("arbitrary", "arbitrary")),
        name="ffn",
    )(x, mod, nw, wg, wu, wd, fw)


def _rope(x, cos, sins):
    n = x.shape[1]
    lane = lax.broadcasted_iota(jnp.int32, (1, n), 1) % MLA_HEAD_PAD
    first_half = (lane >= MLA_NOPE) & (lane < MLA_NOPE + MLA_ROPE // 2)
    partner = jnp.where(first_half, pltpu.roll(x, n - MLA_ROPE // 2, 1), pltpu.roll(x, MLA_ROPE // 2, 1))
    return x * cos + partner * sins


def _proj_kernel(x_ref, mod_ref, nw_ref, win_ref, qn_ref, wuq_ref, kvn_ref, wuk_ref, wuvt_ref, vone_ref,
                 cos_ref, sin_ref, a_ref, u_ref, c_ref, dq_ref, dk_ref, dvt_ref):
    x = x_ref[...]
    h = _modnorm(x, nw_ref[...], mod_ref[3:4, :], mod_ref[4:5, :]).astype(BF16)
    p = jnp.dot(h, win_ref[...], preferred_element_type=F32)
    a_ref[...] = p[:, 0:A_W].astype(BF16)
    u_ref[...] = p[:, U_OFF:U_OFF + GROUP_W].astype(BF16)
    c_ref[:, 0:GROUP_W] = (p[:, C_OFF:C_OFF + GROUP_W] * (NA_DH ** -0.5)).astype(BF16)
    c_ref[:, GROUP_W:3 * GROUP_W] = p[:, C_OFF + GROUP_W:C_OFF + 3 * GROUP_W].astype(BF16)
    qn = _rms(p[:, DQ_OFF:DQ_OFF + MLA_Q_RANK], qn_ref[...]).astype(BF16)
    kvn = _rms(p[:, DKV_OFF:DKV_OFF + MLA_KV_RANK], kvn_ref[...]).astype(BF16)
    q = jnp.dot(qn, wuq_ref[...], preferred_element_type=F32)
    k = jnp.dot(kvn, wuk_ref[...], preferred_element_type=F32) + p[:, KPE_OFF:PROJ_W2]
    vt = lax.dot_general(wuvt_ref[...], kvn, NT_DIMS, preferred_element_type=F32) + vone_ref[...]
    cos = jnp.concatenate([cos_ref[...]] * MLA_HEADS, axis=1)
    sins = jnp.concatenate([sin_ref[...]] * MLA_HEADS, axis=1)
    dq_ref[...] = (_rope(q, cos, sins) * (MLA_SCALE * LOG2E)).astype(BF16)
    dk_ref[...] = _rope(k, cos, sins).astype(BF16)
    dvt_ref[...] = vt.astype(BF16)


def _proj(x, mod, nw, win, qn, wuq, kvn, wuk, wuvt, vone, cos, sins, *, tm=512):
    B, S, D = x.shape
    HP = MLA_HEADS * MLA_HEAD_PAD
    tok = lambda w: pl.BlockSpec((None, tm, w), lambda b, i: (b, i, 0))
    shapes = [A_W, GROUP_W, 3 * GROUP_W, HP, HP]
    return pl.pallas_call(
        _proj_kernel,
        grid=(B, S // tm),
        in_specs=[
            tok(D),
            pl.BlockSpec((None, N_MOD, D), lambda b, i: (b, 0, 0)),
            _const_spec((1, D)),
            _const_spec((D, PROJ_W2)),
            _const_spec((1, MLA_Q_RANK)),
            _const_spec((MLA_Q_RANK, HP)),
            _const_spec((1, MLA_KV_RANK)),
            _const_spec((MLA_KV_RANK, HP)),
            _const_spec((MLA_VT_ROWS, MLA_KV_RANK)),
            _const_spec((MLA_VT_ROWS, 1)),
            pl.BlockSpec((tm, MLA_HEAD_PAD), lambda b, i: (i, 0)),
            pl.BlockSpec((tm, MLA_HEAD_PAD), lambda b, i: (i, 0)),
        ],
        out_specs=[tok(w) for w in shapes] + [pl.BlockSpec((None, MLA_VT_ROWS, tm), lambda b, i: (b, 0, i))],
        out_shape=[jax.ShapeDtypeStruct((B, S, w), BF16) for w in shapes]
        + [jax.ShapeDtypeStruct((B, MLA_VT_ROWS, S), BF16)],
        compiler_params=_cparams(("arbitrary", "arbitrary")),
        name="proj",
    )(x, mod, nw, win, qn, wuq, kvn, wuk, wuvt, vone, cos, sins)


def _log_sigmoid(z):
    return -(jnp.maximum(-z, 0.0) + jnp.log(1.0 + jnp.exp(-jnp.abs(z))))


def _gla_kernel(a_ref, wgk_ref, bgk_ref, gn_ref, o_ref, ofwd_ref, st_ref, *, nt, tt):
    j = pl.program_id(1)
    C = GLA_CHUNK
    nc = tt // C
    KW = GLA_HEADS * GLA_DK
    VW = GLA_HEADS * GLA_DV
    qscale = GLA_DK ** -0.5

    @pl.when((j == 0) | (j == nt))
    def _():
        st_ref[...] = jnp.zeros_like(st_ref)

    row = lax.broadcasted_iota(jnp.int32, (tt, tt), 0)
    col = lax.broadcasted_iota(jnp.int32, (tt, tt), 1)
    cum_mat = jnp.where((row >= col) & (row // C == col // C), 1.0, 0.0).astype(BF16)
    srow = lax.broadcasted_iota(jnp.int32, (GLA_HEADS * C, C), 0) % C
    scol = lax.broadcasted_iota(jnp.int32, (GLA_HEADS * C, C), 1)
    st_mask = (lax.broadcasted_iota(jnp.int32, (VW, KW), 0) // GLA_DV
               == lax.broadcasted_iota(jnp.int32, (VW, KW), 1) // GLA_DK)
    kmasks = [_lane_mask(KW, h * GLA_DK, (h + 1) * GLA_DK) for h in range(GLA_HEADS)]
    vmasks = [_lane_mask(VW, h * GLA_DV, (h + 1) * GLA_DV) for h in range(GLA_HEADS)]

    lr = a_ref[:, A_LR:A_LR + 2 * GLA_GATE_RANK]

    def log_decay(lo):
        z = jnp.dot(lr, wgk_ref[:, lo:lo + KW], preferred_element_type=F32) + bgk_ref[:, lo:lo + KW]
        return _log_sigmoid(z) * (1.0 / GLA_GATE_NORM)

    def chunk_out(q_intra, k_intra, q_inter, k_state, decay, vc, keep):
        qs = jnp.concatenate([jnp.where(m, q_intra, 0.0) for m in kmasks], axis=0).astype(BF16)
        att = lax.dot_general(qs, k_intra.astype(BF16), NT_DIMS, preferred_element_type=F32)
        att = jnp.where(keep, att, 0.0).astype(BF16)
        full = jnp.dot(att, vc, preferred_element_type=F32)
        o = jnp.zeros((C, VW), F32)
        for h in range(GLA_HEADS):
            o = o + jnp.where(vmasks[h], full[h * C:(h + 1) * C, :], 0.0)
        st = st_ref[...]
        o = o + lax.dot_general(q_inter.astype(BF16), st.astype(BF16), NT_DIMS,
                                preferred_element_type=F32)
        kv = lax.dot_general(vc, k_state.astype(BF16), TN_DIMS, preferred_element_type=F32)
        st_ref[...] = st * decay + jnp.where(st_mask, kv, 0.0)
        return o

    @pl.when(j < nt)
    def _():
        g = log_decay(0)
        b = _split_dot(cum_mat, g)
        for c in range(nc):
            r0 = c * C
            bc = b[r0:r0 + C, :]
            bl = b[r0 + C - 1:r0 + C, :]
            qc = a_ref[r0:r0 + C, 0:KW].astype(F32)
            kc = a_ref[r0:r0 + C, KW:2 * KW].astype(F32)
            vc = a_ref[r0:r0 + C, 2 * KW:2 * KW + VW]
            qd = qc * (jnp.exp(bc) * qscale)
            o = chunk_out(qd, kc * jnp.exp(-bc), qd, kc * jnp.exp(bl - bc), jnp.exp(bl), vc,
                          srow >= scol)
            ofwd_ref[pl.ds(pl.multiple_of(j * tt + r0, C), C), :] = o

    @pl.when(j >= nt)
    def _():
        t = 2 * nt - 1 - j
        g = log_decay(KW)
        binc = _split_dot(cum_mat, g)
        e = binc - g
        outs = [None] * nc
        for c in reversed(range(nc)):
            r0 = c * C
            ec = e[r0:r0 + C, :]
            gt = binc[r0 + C - 1:r0 + C, :]
            qc = a_ref[r0:r0 + C, 0:KW].astype(F32)
            kc = a_ref[r0:r0 + C, KW:2 * KW].astype(F32)
            vc = a_ref[r0:r0 + C, 2 * KW:2 * KW + VW]
            ke = kc * jnp.exp(ec)
            o = chunk_out(qc * (jnp.exp(-ec) * qscale), ke, qc * (jnp.exp(gt - ec) * qscale), ke,
                          jnp.exp(gt), vc, srow < scol)
            outs[c] = o + ofwd_ref[pl.ds(pl.multiple_of(t * tt + r0, C), C), :]
        o = jnp.concatenate(outs, axis=0)
        seg = jnp.where(lax.broadcasted_iota(jnp.int32, (VW, VW), 0) // GLA_DV
                        == lax.broadcasted_iota(jnp.int32, (VW, VW), 1) // GLA_DV, 1.0, 0.0).astype(BF16)
        ms = _split_dot_rhs(o * o, seg) * (1.0 / GLA_DV)
        gout = a_ref[:, 2 * KW + VW:2 * KW + 2 * VW].astype(F32)
        o_ref[...] = (o * lax.rsqrt(ms + EPS) * gn_ref[...] * _silu(gout)).astype(BF16)


def _split_dot_rhs(x, ones_mat):
    hi = x.astype(BF16)
    lo = (x - hi.astype(F32)).astype(BF16)
    return (jnp.dot(hi, ones_mat, preferred_element_type=F32)
            + jnp.dot(lo, ones_mat, preferred_element_type=F32))


def _gla(a, wgk, bgk, gn, *, tt=256):
    B, S, _ = a.shape
    nt = S // tt
    VW = GLA_HEADS * GLA_DV
    KW = GLA_HEADS * GLA_DK
    in_tile = lambda b, j: (b, jnp.where(j < nt, j, 2 * nt - 1 - j), 0)
    out_tile = lambda b, j: (b, jnp.where(j < nt, nt - 1, 2 * nt - 1 - j), 0)
    return pl.pallas_call(
        functools.partial(_gla_kernel, nt=nt, tt=tt),
        grid=(B, 2 * nt),
        in_specs=[
            pl.BlockSpec((None, tt, A_W), in_tile),
            pl.BlockSpec((2 * GLA_GATE_RANK, 2 * KW), lambda b, j: (0, 0)),
            pl.BlockSpec((1, 2 * KW), lambda b, j: (0, 0)),
            pl.BlockSpec((1, VW), lambda b, j: (0, 0)),
        ],
        out_specs=pl.BlockSpec((None, tt, VW), out_tile),
        out_shape=jax.ShapeDtypeStruct((B, S, VW), BF16),
        scratch_shapes=[pltpu.VMEM((S, VW), F32), pltpu.VMEM((VW, KW), F32)],
        compiler_params=_cparams(("arbitrary", "arbitrary")),
        name="gla",
    )(a, wgk, bgk, gn)


def _na_kernel(q_ref, k_ref, v_ref, bias_ref, o_ref, *, rows):
    r = pl.program_id(1)
    nk = NA_KROWS * GRID_W
    ks = pl.multiple_of(jnp.clip(r * NA_QROWS - NA_WIN_R // 2, 0, rows - NA_KROWS) * GRID_W, GRID_W)
    q = q_ref[...]
    k = k_ref[pl.ds(ks, nk), :]
    v = v_ref[pl.ds(ks, nk), :]
    W = NA_HEADS * NA_DH
    acc = jnp.zeros((NA_QROWS * GRID_W, W), F32)
    for h in range(NA_HEADS):
        lm = _lane_mask(W, h * NA_DH, (h + 1) * NA_DH)
        qh = jnp.where(lm, q, jnp.zeros_like(q))
        s = lax.dot_general(qh, k, NT_DIMS, preferred_element_type=F32) + bias_ref[h]
        m = jnp.max(s, axis=-1, keepdims=True)
        p = jnp.exp(s - m)
        l = jnp.sum(p, axis=-1, keepdims=True)
        oh = jnp.dot(p.astype(BF16), v, preferred_element_type=F32)
        acc = acc + jnp.where(lm, oh / l, 0.0)
    o_ref[...] = acc.astype(BF16)


def _na_bias_tables(rpb, rows):
    nr = rows // NA_QROWS
    H, ndr, ndc = rpb.shape
    period = 2 * GRID_W
    gap = jnp.full((H, ndr, period - ndc), NEG_INF, F32)
    wpad = jnp.concatenate([rpb[..., NA_WIN_C - 1:].astype(F32), gap, rpb[..., :NA_WIN_C - 1].astype(F32)], -1)
    skew = jnp.tile(wpad, (1, 1, GRID_W))[..., :GRID_W * (period - 1)]
    blocks = skew.reshape(H, ndr, GRID_W, period - 1)[..., :GRID_W]
    qc = np.arange(GRID_W)[:, None]
    kc = np.arange(GRID_W)[None, :]
    c0 = np.clip(qc - NA_WIN_C // 2, 0, GRID_W - NA_WIN_C)
    col_ok = (kc >= c0) & (kc < c0 + NA_WIN_C)
    blocks = jnp.where(col_ok[None, None], blocks, NEG_INF)
    masked = jnp.full((H, GRID_W, GRID_W), NEG_INF, F32)
    tables = []
    for rb in (0, 1, nr - 1):
        ks = int(np.clip(rb * NA_QROWS - NA_WIN_R // 2, 0, rows - NA_KROWS))
        qrows = []
        for i in range(NA_QROWS):
            qr = rb * NA_QROWS + i
            r0 = int(np.clip(qr - NA_WIN_R // 2, 0, rows - NA_WIN_R))
            parts = []
            for jj in range(NA_KROWS):
                kr = ks + jj
                parts.append(blocks[:, kr - qr + NA_WIN_R - 1] if r0 <= kr < r0 + NA_WIN_R else masked)
            qrows.append(jnp.concatenate(parts, axis=-1))
        tables.append(jnp.concatenate(qrows, axis=-2))
    return jnp.stack(tables)


def _na(c, bias, *, rows):
    B, S, _ = c.shape
    nr = rows // NA_QROWS
    W = NA_HEADS * NA_DH
    nq = NA_QROWS * GRID_W
    nk = NA_KROWS * GRID_W
    variant = lambda b, r: (jnp.where(r == 0, 0, jnp.where(r == nr - 1, 2, 1)), 0, 0, 0)
    return pl.pallas_call(
        functools.partial(_na_kernel, rows=rows),
        grid=(B, nr),
        in_specs=[
            pl.BlockSpec((None, nq, W), lambda b, r: (b, r, 0)),
            pl.BlockSpec((None, S, W), lambda b, r: (b, 0, 1)),
            pl.BlockSpec((None, S, W), lambda b, r: (b, 0, 2)),
            pl.BlockSpec((None, NA_HEADS, nq, nk), variant),
        ],
        out_specs=pl.BlockSpec((None, nq, W), lambda b, r: (b, r, 0)),
        out_shape=jax.ShapeDtypeStruct((B, S, W), BF16),
        compiler_params=_cparams(("arbitrary", "arbitrary")),
        name="na",
    )(c, c, c, bias)


def _mla_kernel(q_ref, k_ref, vt_ref, o_ref, *, tk):
    S = k_ref.shape[0]
    items = [(h, c0) for h in range(MLA_HEADS) for c0 in range(0, S, tk)]

    def scores(h, c0):
        sl = slice(h * MLA_HEAD_PAD, (h + 1) * MLA_HEAD_PAD)
        return lax.dot_general(k_ref[c0:c0 + tk, sl], q_ref[:, sl], NT_DIMS, preferred_element_type=F32)

    st_next = scores(*items[0])
    m = acc = None
    for n, (h, c0) in enumerate(items):
        st = st_next
        if n + 1 < len(items):
            st_next = scores(*items[n + 1])
        mc = jnp.max(st, axis=0, keepdims=True)
        m_new = mc if c0 == 0 else jnp.maximum(m, mc)
        p = jnp.exp2((st - m_new).astype(BF16))
        pv = jnp.dot(vt_ref[h * MLA_VT_HEAD:(h + 1) * MLA_VT_HEAD, c0:c0 + tk], p, preferred_element_type=F32)
        acc = pv if c0 == 0 else acc * jnp.exp2(m - m_new) + pv
        m = m_new
        if c0 + tk == S:
            o_ref[h * MLA_V:(h + 1) * MLA_V, :] = (acc[0:MLA_V, :] / acc[MLA_V:MLA_V + 1, :]).astype(BF16)


def _mla(dq, dk, dvt, *, tq=256, tk=2048):
    B, S, HP = dq.shape
    W = MLA_HEADS * MLA_V
    return pl.pallas_call(
        functools.partial(_mla_kernel, tk=min(tk, S)),
        grid=(B, S // tq),
        in_specs=[
            pl.BlockSpec((None, tq, HP), lambda b, i: (b, i, 0)),
            pl.BlockSpec((None, S, HP), lambda b, i: (b, 0, 0)),
            pl.BlockSpec((None, MLA_VT_ROWS, S), lambda b, i: (b, 0, 0)),
        ],
        out_specs=pl.BlockSpec((None, W, tq), lambda b, i: (b, 0, i)),
        out_shape=jax.ShapeDtypeStruct((B, W, S), BF16),
        compiler_params=_cparams(("arbitrary", "arbitrary")),
        name="mla",
    )(dq, dk, dvt)


def _out_kernel(x_ref, mod_ref, ya_ref, u_ref, yc_ref, ydt_ref, wp_ref, ps_ref, wo_ref, o_ref, *, S, tm):
    i = pl.program_id(1)
    t0 = i * tm
    ne = tm + 2 * POOL_HALO
    e0 = pl.multiple_of(jnp.clip(t0 - POOL_HALO, 0, S - ne), POOL_HALO)
    ue = u_ref[pl.ds(e0, ne), :]
    ut = u_ref[pl.ds(pl.multiple_of(t0, tm), tm), :].astype(F32)
    tok = t0 + lax.broadcasted_iota(jnp.int32, (tm, ne), 0)
    src = e0 + lax.broadcasted_iota(jnp.int32, (tm, ne), 1)
    tcol = t0 + lax.broadcasted_iota(jnp.int32, (tm, 1), 0)
    mean = jnp.zeros((tm, GROUP_W), F32)
    for gi, w in enumerate(POOL_WINDOWS):
        lo = jnp.maximum(tok - w // 2, 0)
        hi = jnp.minimum(tok + w // 2, S)
        band = jnp.where(src >= lo, jnp.where(src < hi, 1.0, 0.0), 0.0).astype(BF16)
        cnt = (jnp.minimum(tcol + w // 2, S) - jnp.maximum(tcol - w // 2, 0)).astype(F32)
        sm = jnp.dot(band, ue, preferred_element_type=F32)
        mean = mean + jnp.where(_lane_mask(GROUP_W, gi * POOL_GW, (gi + 1) * POOL_GW), sm / cnt, 0.0)
    d = (mean - ut).astype(BF16)
    yb = (jnp.dot(d, wp_ref[...], preferred_element_type=F32) * ps_ref[...]).astype(BF16)
    y = jnp.concatenate([ya_ref[...], yb, yc_ref[...]], axis=1)
    t = jnp.dot(y, wo_ref[0:3 * GROUP_W, :], preferred_element_type=F32)
    t = t + lax.dot_general(ydt_ref[...], wo_ref[3 * GROUP_W:4 * GROUP_W, :], TN_DIMS,
                            preferred_element_type=F32)
    o_ref[...] = x_ref[...] + (1.0 + mod_ref[5:6, :]) * t


def _out(x, mod, ya, u, yc, ydt, wp, ps, wo, *, tm=256):
    B, S, D = x.shape
    tok = lambda w: pl.BlockSpec((None, tm, w), lambda b, i: (b, i, 0))
    return pl.pallas_call(
        functools.partial(_out_kernel, S=S, tm=tm),
        grid=(B, S // tm),
        in_specs=[
            tok(D),
            pl.BlockSpec((None, N_MOD, D), lambda b, i: (b, 0, 0)),
            tok(GROUP_W),
            pl.BlockSpec((None, S, GROUP_W), lambda b, i: (b, 0, 0)),
            tok(GROUP_W),
            pl.BlockSpec((None, GROUP_W, tm), lambda b, i: (b, 0, i)),
            _const_spec((GROUP_W, GROUP_W)),
            _const_spec((1, GROUP_W)),
            _const_spec((D, D)),
        ],
        out_specs=tok(D),
        out_shape=jax.ShapeDtypeStruct((B, S, D), F32),
        compiler_params=_cparams(("arbitrary", "arbitrary")),
        name="out",
    )(x, mod, ya, u, yc, ydt, wp, ps, wo)


def _take_cols(w, idx):
    parts = []
    start = 0
    for end in range(1, len(idx) + 1):
        same_run = end < len(idx) and (
            (idx[end] < 0 and idx[end - 1] < 0) or (idx[end - 1] >= 0 and idx[end] == idx[end - 1] + 1))
        if not same_run:
            if idx[start] < 0:
                parts.append(jnp.zeros((w.shape[0], end - start), w.dtype))
            else:
                parts.append(w[:, int(idx[start]):int(idx[start]) + end - start])
            start = end
    return jnp.concatenate(parts, axis=1)


def _win_index():
    idx = np.full((PROJ_W2,), -1, np.int64)
    idx[0:800] = np.arange(0, 800)
    idx[U_OFF:U_OFF + 256] = np.arange(800, 1056)
    idx[C_OFF:C_OFF + 768] = np.arange(1056, 1824)
    idx[DQ_OFF:DQ_OFF + 256] = np.arange(1824, 2080)
    idx[DKV_OFF:DKV_OFF + 128] = np.arange(2080, 2208)
    for h in range(MLA_HEADS):
        lo = KPE_OFF + h * MLA_HEAD_PAD + MLA_NOPE
        idx[lo:lo + MLA_ROPE] = np.arange(2208, 2240)
    return idx


def _head_pad_index(per_head, keep, src_off=0):
    idx = np.full((MLA_HEADS * MLA_HEAD_PAD,), -1, np.int64)
    for h in range(MLA_HEADS):
        idx[h * MLA_HEAD_PAD:h * MLA_HEAD_PAD + keep] = h * per_head + src_off + np.arange(keep)
    return idx


def _layer_weights(l, P):
    bf = lambda t: t.astype(BF16)
    row = lambda t: t.reshape(1, -1).astype(F32)
    w = {}
    w["nf1"], w["nmix"], w["nf2"] = row(P["norm_ffn1"][l]), row(P["norm_mix"][l]), row(P["norm_ffn2"][l])
    for n in ("ffn1_wg", "ffn1_wu", "ffn1_wd", "ffn2_wg", "ffn2_wu", "ffn2_wd", "w_out"):
        w[n] = bf(P[n][l])
    w["win"] = bf(_take_cols(P["w_in"][l], _win_index()))
    w["qn"], w["kvn"] = row(P["mla_qnorm"][l]), row(P["mla_kvnorm"][l])
    w["wuq"] = bf(_take_cols(P["mla_wuq"][l], _head_pad_index(MLA_NOPE + MLA_ROPE, MLA_NOPE + MLA_ROPE)))
    w["wuk"] = bf(_take_cols(P["mla_wukv"][l], _head_pad_index(MLA_NOPE + MLA_V, MLA_NOPE)))
    vidx = np.full((MLA_VT_ROWS,), -1, np.int64)
    vone = np.zeros((MLA_VT_ROWS, 1), np.float32)
    for h in range(MLA_HEADS):
        vidx[h * MLA_VT_HEAD:h * MLA_VT_HEAD + MLA_V] = h * (MLA_NOPE + MLA_V) + MLA_NOPE + np.arange(MLA_V)
        vone[h * MLA_VT_HEAD + MLA_V:(h + 1) * MLA_VT_HEAD] = 1.0
    w["wuvt"] = bf(_take_cols(P["mla_wukv"][l], vidx).T)
    w["vone"] = jnp.asarray(vone)
    KW = GLA_HEADS * GLA_DK
    wgk = jnp.zeros((2 * GLA_GATE_RANK, 2 * KW), F32)
    wgk = wgk.at[:GLA_GATE_RANK, :KW].set(P["gla_wgk_f"][l]).at[GLA_GATE_RANK:, KW:].set(P["gla_wgk_b"][l])
    w["wgk"] = bf(wgk)
    w["bgk"] = row(jnp.concatenate([P["gla_bgk_f"][l], P["gla_bgk_b"][l]]))
    w["gn"] = row(jnp.tile(P["gla_norm"][l], GLA_HEADS))
    wp = jnp.zeros((GROUP_W, GROUP_W), F32)
    for gi in range(len(POOL_WINDOWS)):
        sl = slice(gi * POOL_GW, (gi + 1) * POOL_GW)
        wp = wp.at[sl, sl].set(P["pool_w"][l][gi])
    w["wp"] = bf(wp)
    w["ps"] = row(P["pool_scale"][l])
    w["rpb"] = P["na_rpb"][l]
    return w


def _rope_tables(S):
    half = MLA_ROPE // 2
    inv = ROPE_THETA ** (-jnp.arange(half, dtype=F32) / half)
    ang = jnp.arange(S, dtype=F32)[:, None] * inv[None, :]
    cos, sin = jnp.cos(ang), jnp.sin(ang)
    ones = jnp.ones((S, MLA_NOPE), F32)
    pad = jnp.zeros((S, MLA_HEAD_PAD - MLA_NOPE - MLA_ROPE), F32)
    cos_t = jnp.concatenate([ones, cos, cos, pad], axis=1)
    sin_t = jnp.concatenate([jnp.zeros((S, MLA_NOPE), F32), -sin, sin, pad], axis=1)
    return cos_t, sin_t


def _run_stream(x, mods, W, final_w):
    B, S, D = x.shape
    rows = S // GRID_W
    cos_t, sin_t = _rope_tables(S)
    for l in range(DEPTH):
        w = W[l]
        mod = mods[l]
        x = _ffn(x, mod, w["nf1"], w["ffn1_wg"], w["ffn1_wu"], w["ffn1_wd"], final_w, mi=0, final=False)
        a, u, c, dq, dk, dvt = _proj(x, mod, w["nmix"], w["win"], w["qn"], w["wuq"], w["kvn"], w["wuk"],
                                     w["wuvt"], w["vone"], cos_t, sin_t)
        ya = _gla(a, w["wgk"], w["bgk"], w["gn"])
        yc = _na(c, _na_bias_tables(w["rpb"], rows), rows=rows)
        ydt = _mla(dq, dk, dvt)
        x = _out(x, mod, ya, u, yc, ydt, w["wp"], w["ps"], w["w_out"])
        x = _ffn(x, mod, w["nf2"], w["ffn2_wg"], w["ffn2_wu"], w["ffn2_wd"], final_w, mi=6,
                 final=(l == DEPTH - 1))
    return x


def kernel(x_prompt, x_sample, c_prompt, c_sample, ada_w, ada_b, norm_ffn1, ffn1_wg, ffn1_wu, ffn1_wd, norm_mix, w_in, gla_wgk_f, gla_bgk_f, gla_wgk_b, gla_bgk_b, gla_norm, pool_w, pool_scale, na_rpb, mla_qnorm, mla_wuq, mla_kvnorm, mla_wukv, w_out, norm_ffn2, ffn2_wg, ffn2_wu, ffn2_wd, final_norm):
    P = dict(norm_ffn1=norm_ffn1, ffn1_wg=ffn1_wg, ffn1_wu=ffn1_wu, ffn1_wd=ffn1_wd, norm_mix=norm_mix,
             w_in=w_in, gla_wgk_f=gla_wgk_f, gla_bgk_f=gla_bgk_f, gla_wgk_b=gla_wgk_b, gla_bgk_b=gla_bgk_b,
             gla_norm=gla_norm, pool_w=pool_w, pool_scale=pool_scale, na_rpb=na_rpb, mla_qnorm=mla_qnorm,
             mla_wuq=mla_wuq, mla_kvnorm=mla_kvnorm, mla_wukv=mla_wukv, w_out=w_out, norm_ffn2=norm_ffn2,
             ffn2_wg=ffn2_wg, ffn2_wu=ffn2_wu, ffn2_wd=ffn2_wd)
    W = [_layer_weights(l, P) for l in range(DEPTH)]
    final_w = final_norm.reshape(1, -1).astype(F32)
    bp = x_prompt.shape[0]
    c_all = jnp.concatenate([c_prompt, c_sample], axis=0)
    mod_all = _modulation(c_all, ada_w, ada_b)
    mod_all = mod_all.reshape(DEPTH, c_all.shape[0], N_MOD, D_MODEL)
    mods_p = [mod_all[l, :bp] for l in range(DEPTH)]
    mods_s = [mod_all[l, bp:] for l in range(DEPTH)]
    y_prompt = _run_stream(x_prompt, mods_p, W, final_w)
    y_sample = _run_stream(x_sample, mods_s, W, final_w)
    return (y_prompt, y_sample)
```
